```python
import math
import jax, jax.numpy as jnp
from jax import lax
import numpy as np

D_MODEL = 2048
BATCH = 4
SEQ = 2048
DEPTH = 4

MLA_HEADS = 8
MLA_Q_LORA = 512
MLA_KV_LORA = 512
MLA_NOPE_DIM = 128
MLA_ROPE_DIM = 64
MLA_V_DIM = 128
MLA_QK_DIM = MLA_NOPE_DIM + MLA_ROPE_DIM
ATTN_Q_BLOCK = 128

POOL_WIDTH = 1024
POOL_WINDOWS = (2, 4, 8, 16)
POOL_GROUP = POOL_WIDTH // len(POOL_WINDOWS)

CONV_WIDTH = 1024
CONV_KSIZE = 31

DIL_CONFIGS = ((128, 1), (512, 4), (2048, 16))
DIL_HEADS = 8
DIL_HEAD_DIM = 128
DIL_WIDTH = DIL_HEADS * DIL_HEAD_DIM
DIL_ROT_DIM = DIL_HEAD_DIM // 4

N_BRANCHES = 4
ROPE_THETA = 500000.0

FFN_DIM = 5632
FFN_CONV_KSIZE = 3

ALPHA = (2 * DEPTH) ** 0.25
BETA = (8 * DEPTH) ** -0.25

LN_EPS = 1e-5
RMS_EPS = 1e-6
NEG_INF = -1e30

_SPLITS = (MLA_Q_LORA, MLA_KV_LORA, MLA_ROPE_DIM, POOL_WIDTH, 2 * CONV_WIDTH,
           len(DIL_CONFIGS) * 3 * DIL_WIDTH, N_BRANCHES * D_MODEL)
IN_COLS = sum(_SPLITS)

kernel_name = "hybrid_gated_mla_pool_conv_dilated_deepnorm"


def layer_norm(x, g, b):
    xf = x.astype(jnp.float32)
    mu = jnp.mean(xf, axis=-1, keepdims=True)
    var = jnp.mean(jnp.square(xf - mu), axis=-1, keepdims=True)
    return ((xf - mu) * lax.rsqrt(var + LN_EPS) * g + b).astype(x.dtype)


def rms_norm(x, g):
    xf = x.astype(jnp.float32)
    return (xf * lax.rsqrt(jnp.mean(jnp.square(xf), axis=-1, keepdims=True) + RMS_EPS) * g).astype(x.dtype)


def rope(x, positions, rot_dim):
    half = rot_dim // 2
    inv_freq = ROPE_THETA ** (-jnp.arange(half, dtype=jnp.float32) * 2.0 / rot_dim)
    ang = positions.astype(jnp.float32)[..., None] * inv_freq
    if x.ndim == 4:
        ang = ang[:, :, None, :]
    cos, sin = jnp.cos(ang), jnp.sin(ang)
    xr = x[..., :rot_dim].astype(jnp.float32)
    x1, x2 = xr[..., :half], xr[..., half:]
    rot = jnp.concatenate([x1 * cos - x2 * sin, x2 * cos + x1 * sin], axis=-1)
    return jnp.concatenate([rot.astype(x.dtype), x[..., rot_dim:]], axis=-1)


def causal_dwconv(x, w, b):
    k = w.shape[0]
    y = lax.conv_general_dilated(
        x, w[:, None, :].astype(x.dtype), window_strides=(1,), padding=[(k - 1, 0)],
        dimension_numbers=("NWC", "WIO", "NWC"), feature_group_count=x.shape[-1])
    return y + b


def mla_attention(q_nope, q_pe, k_nope, k_pe, v):
    S = q_nope.shape[1]
    scale = MLA_QK_DIM ** -0.5
    outs = []
    for n in range(S // ATTN_Q_BLOCK):
        s0, s1 = n * ATTN_Q_BLOCK, (n + 1) * ATTN_Q_BLOCK
        sc = (jnp.einsum("bqhd,bkhd->bhqk", q_nope[:, s0:s1], k_nope[:, :s1])
              + jnp.einsum("bqhd,bkd->bhqk", q_pe[:, s0:s1], k_pe[:, :s1]))
        sc = sc.astype(jnp.float32) * scale
        mask = jnp.arange(s1)[None, :] <= jnp.arange(s0, s1)[:, None]
        sc = jnp.where(mask, sc, NEG_INF)
        p = jax.nn.softmax(sc, axis=-1).astype(v.dtype)
        outs.append(jnp.einsum("bhqk,bkhd->bqhd", p, v[:, :s1]))
    return jnp.concatenate(outs, axis=1)


def dilated_window_attention(q, k, v, window, dilation):
    B, S, H, D = q.shape
    nb = window // dilation
    L = S // dilation
    nblk = -(-L // nb)
    Lp = nblk * nb
    Bz = B * dilation

    def to_sub(t):
        t = t.reshape(B, L, dilation, H, D).transpose(0, 2, 1, 3, 4).reshape(Bz, L, H, D)
        t = jnp.pad(t, ((0, 0), (0, Lp - L), (0, 0), (0, 0)))
        return t.reshape(Bz, nblk, nb, H, D)

    def with_prev(t):
        prev = jnp.concatenate([jnp.zeros_like(t[:, :1]), t[:, :-1]], axis=1)
        return jnp.concatenate([prev, t], axis=2)

    qs = to_sub(q)
    kb = with_prev(to_sub(k))
    vb = with_prev(to_sub(v))
    sc = jnp.einsum("znqhd,znkhd->znhqk", qs, kb).astype(jnp.float32) * (D ** -0.5)
    blk = jnp.arange(nblk)[:, None, None] * nb
    qi = blk + jnp.arange(nb)[None, :, None]
    kj = blk - nb + jnp.arange(2 * nb)[None, None, :]
    rel = qi - kj
    valid = (rel >= 0) & (rel <= nb) & (kj >= 0)
    sc = jnp.where(valid[None, :, None], sc, NEG_INF)
    lse = jax.nn.logsumexp(sc, axis=-1)
    p = jnp.exp(sc - lse[..., None]).astype(v.dtype)
    o = jnp.einsum("znhqk,znkhd->znqhd", p, vb)
    o = o.reshape(Bz, Lp, H, D)[:, :L]
    o = o.reshape(B, dilation, L, H, D).transpose(0, 2, 1, 3, 4).reshape(B, S, H, D)
    lse = lse.transpose(0, 1, 3, 2).reshape(Bz, Lp, H)[:, :L]
    lse = lse.reshape(B, dilation, L, H).transpose(0, 2, 1, 3).reshape(B, S, H)
    return o, lse


def multiscale_pool(p, w_grp, scale):
    B, S, C = p.shape
    pf = p.astype(jnp.float32)
    cs = jnp.concatenate([jnp.zeros((B, 1, C), jnp.float32), jnp.cumsum(pf, axis=1)], axis=1)
    t = jnp.arange(S)
    diffs = []
    for g, w in enumerate(POOL_WINDOWS):
        sl = slice(g * POOL_GROUP, (g + 1) * POOL_GROUP)
        lo = jnp.maximum(t + 1 - w, 0)
        cnt = jnp.minimum(t + 1, w).astype(jnp.float32)[None, :, None]
        mean = (cs[:, 1:, sl] - cs[:, lo, sl]) / cnt
        diffs.append(mean - pf[..., sl])
    d = jnp.stack(diffs, axis=2).astype(p.dtype)
    y = jnp.einsum("bsgi,gio->bsgo", d, w_grp).reshape(B, S, C)
    return y * scale


def conformer_conv(u, w_dw, b_dw, ln_g, ln_b):
    a, b = jnp.split(u, 2, axis=-1)
    h = causal_dwconv(a * jax.nn.sigmoid(b), w_dw, b_dw)
    return jax.nn.silu(layer_norm(h, ln_g, ln_b))


def hybrid_mixer(x, positions, w_in, b_gate, mla_gq, mla_gkv, mla_w_uq, mla_w_ukv, mla_w_proj,
                 pool_w, pool_scale, pool_w_proj, conv_dw, conv_dw_b, conv_ln_g, conv_ln_b,
                 conv_w_proj, dil_w_proj, mix_w_out):
    B, S, _ = x.shape
    u = x @ w_in
    split_pts = [int(i) for i in np.cumsum(_SPLITS)[:-1]]
    c_q, c_kv, k_pe, u_pool, u_conv, u_dil, u_gate = jnp.split(u, split_pts, axis=-1)

    q = (rms_norm(c_q, mla_gq) @ mla_w_uq).reshape(B, S, MLA_HEADS, MLA_QK_DIM)
    q_nope = q[..., :MLA_NOPE_DIM]
    q_pe = rope(q[..., MLA_NOPE_DIM:], positions, MLA_ROPE_DIM)
    kv = (rms_norm(c_kv, mla_gkv) @ mla_w_ukv).reshape(B, S, MLA_HEADS, MLA_NOPE_DIM + MLA_V_DIM)
    k_nope, v_a = kv[..., :MLA_NOPE_DIM], kv[..., MLA_NOPE_DIM:]
    k_pe = rope(k_pe, positions, MLA_ROPE_DIM)
    y_a = mla_attention(q_nope, q_pe, k_nope, k_pe, v_a).reshape(B, S, MLA_HEADS * MLA_V_DIM) @ mla_w_proj

    y_b = multiscale_pool(u_pool, pool_w, pool_scale) @ pool_w_proj

    y_c = conformer_conv(u_conv, conv_dw, conv_dw_b, conv_ln_g, conv_ln_b) @ conv_w_proj

    qkv = u_dil.reshape(B, S, len(DIL_CONFIGS), 3, DIL_HEADS, DIL_HEAD_DIM)
    outs, lses = [], []
    for g, (window, dilation) in enumerate(DIL_CONFIGS):
        q_g = rope(qkv[:, :, g, 0], positions, DIL_ROT_DIM)
        k_g = rope(qkv[:, :, g, 1], positions, DIL_ROT_DIM)
        o_g, l_g = dilated_window_attention(q_g, k_g, qkv[:, :, g, 2], window, dilation)
        outs.append(o_g)
        lses.append(l_g)
    wts = jax.nn.softmax(jnp.stack(lses, axis=0), axis=0)
    o_d = jnp.sum(wts[..., None].astype(outs[0].dtype) * jnp.stack(outs, axis=0), axis=0)
    y_d = o_d.reshape(B, S, DIL_WIDTH) @ dil_w_proj

    gates = jax.nn.sigmoid(u_gate.reshape(B, S, N_BRANCHES, D_MODEL) + b_gate)
    merged = (gates[:, :, 0] * y_a + gates[:, :, 1] * y_b
              + gates[:, :, 2] * y_c + gates[:, :, 3] * y_d)
    return merged @ mix_w_out


def conv_ffn(x, w_up, w_dw, b_dw, w_down):
    h = causal_dwconv(x @ w_up, w_dw, b_dw)
    val, gate = jnp.split(h, 2, axis=-1)
    return (val * jax.nn.silu(gate)) @ w_down


def setup_inputs(seed: int = 0) -> dict:
    key = jax.random.key(seed)
    ks = jax.random.split(key, 32)
    f32 = jnp.float32
    L = DEPTH

    def nrm(k, shape, scale):
        return jax.random.normal(k, shape, f32) * scale

    def gain(k, shape):
        return 1.0 + 0.02 * jax.random.normal(k, shape, f32)

    return {
        "x": jax.random.normal(ks[0], (BATCH, SEQ, D_MODEL), f32),
        "positions": jnp.tile(jnp.arange(SEQ, dtype=jnp.int32)[None, :], (BATCH, 1)),
        "w_in": nrm(ks[1], (L, D_MODEL, IN_COLS), D_MODEL ** -0.5),
        "b_gate": nrm(ks[2], (L, N_BRANCHES, D_MODEL), 0.02),
        "mla_gq": gain(ks[3], (L, MLA_Q_LORA)),
        "mla_gkv": gain(ks[4], (L, MLA_KV_LORA)),
        "mla_w_uq": nrm(ks[5], (L, MLA_Q_LORA, MLA_HEADS * MLA_QK_DIM), MLA_Q_LORA ** -0.5),
        "mla_w_ukv": nrm(ks[6], (L, MLA_KV_LORA, MLA_HEADS * (MLA_NOPE_DIM + MLA_V_DIM)), MLA_KV_LORA ** -0.5),
        "mla_w_proj": nrm(ks[7], (L, MLA_HEADS * MLA_V_DIM, D_MODEL), (MLA_HEADS * MLA_V_DIM) ** -0.5 * BETA),
        "pool_w": nrm(ks[8], (L, len(POOL_WINDOWS), POOL_GROUP, POOL_GROUP), POOL_GROUP ** -0.5),
        "pool_scale": gain(ks[9], (L, POOL_WIDTH)),
        "pool_w_proj": nrm(ks[10], (L, POOL_WIDTH, D_MODEL), POOL_WIDTH ** -0.5 * BETA),
        "conv_dw": nrm(ks[11], (L, CONV_KSIZE, CONV_WIDTH), CONV_KSIZE ** -0.5),
        "conv_dw_b": nrm(ks[12], (L, CONV_WIDTH), 0.02),
        "conv_ln_g": gain(ks[13], (L, CONV_WIDTH)),
        "conv_ln_b": nrm(ks[14], (L, CONV_WIDTH), 0.02),
        "conv_w_proj": nrm(ks[15], (L, CONV_WIDTH, D_MODEL), CONV_WIDTH ** -0.5 * BETA),
        "dil_w_proj": nrm(ks[16], (L, DIL_WIDTH, D_MODEL), DIL_WIDTH ** -0.5 * BETA),
        "mix_w_out": nrm(ks[17], (L, D_MODEL, D_MODEL), D_MODEL ** -0.5 * BETA),
        "ln1_g": gain(ks[18], (L, D_MODEL)),
        "ln1_b": nrm(ks[19], (L, D_MODEL), 0.02),
        "ffn_w_up": nrm(ks[20], (L, D_MODEL, 2 * FFN_DIM), D_MODEL ** -0.5),
        "ffn_dw": nrm(ks[21], (L, FFN_CONV_KSIZE, 2 * FFN_DIM), FFN_CONV_KSIZE ** -0.5),
        "ffn_dw_b": nrm(ks[22], (L, 2 * FFN_DIM), 0.02),
        "ffn_w_down": nrm(ks[23], (L, FFN_DIM, D_MODEL), FFN_DIM ** -0.5 * BETA),
        "ln2_g": gain(ks[24], (L, D_MODEL)),
        "ln2_b": nrm(ks[25], (L, D_MODEL), 0.02),
    }


def reference(x, positions, w_in, b_gate, mla_gq, mla_gkv, mla_w_uq, mla_w_ukv, mla_w_proj,
              pool_w, pool_scale, pool_w_proj, conv_dw, conv_dw_b, conv_ln_g, conv_ln_b,
              conv_w_proj, dil_w_proj, mix_w_out, ln1_g, ln1_b, ffn_w_up, ffn_dw, ffn_dw_b,
              ffn_w_down, ln2_g, ln2_b):
    for l in range(DEPTH):
        m = hybrid_mixer(x, positions, w_in[l], b_gate[l], mla_gq[l], mla_gkv[l], mla_w_uq[l],
                         mla_w_ukv[l], mla_w_proj[l], pool_w[l], pool_scale[l], pool_w_proj[l],
                         conv_dw[l], conv_dw_b[l], conv_ln_g[l], conv_ln_b[l], conv_w_proj[l],
                         dil_w_proj[l], mix_w_out[l])
        x = layer_norm(ALPHA * x + m, ln1_g[l], ln1_b[l])
        f = conv_ffn(x, ffn_w_up[l], ffn_dw[l], ffn_dw_b[l], ffn_w_down[l])
        x = layer_norm(ALPHA * x + f, ln2_g[l], ln2_b[l])
    return x
```

```python
import functools

import jax
import jax.numpy as jnp
from jax import lax
from jax.experimental import pallas as pl
from jax.experimental.pallas import tpu as pltpu

MLA_HEADS = 8
MLA_LORA = 512
MLA_NOPE = 128
MLA_ROPE = 64
MLA_V = 128
MLA_QK = MLA_NOPE + MLA_ROPE
POOL_WINDOWS = (2, 4, 8, 16)
POOL_GROUP = 256
POOL_WIDTH = 1024
CONV_WIDTH = 1024
CONV_K = 31
DIL_CONFIGS = ((128, 1), (512, 4), (2048, 16))
DIL_HEADS = 8
DIL_DIM = 128
DIL_WIDTH = DIL_HEADS * DIL_DIM
DIL_ROT = DIL_DIM // 4
DIL_BLOCK = 128
N_BRANCHES = 4
ROPE_THETA = 500000.0
FFN_DIM = 5632
FFN_K = 3
LN_EPS = 1e-5
RMS_EPS = 1e-6
NEG_INF = -1e30

LANES = 128
V7X_VMEM_BYTES = 64 * 1024 * 1024
VMEM_CAP = V7X_VMEM_BYTES - 8 * 1024 * 1024

BF16 = jnp.bfloat16
F32 = jnp.float32


def _params(semantics, block_bytes):
    limit = min(VMEM_CAP, max(32 * 1024 * 1024, 2 * block_bytes + 16 * 1024 * 1024))
    return pltpu.CompilerParams(dimension_semantics=semantics, vmem_limit_bytes=limit)


def _nbytes(shape, dtype):
    n = 1
    for s in shape:
        n *= s
    return n * jnp.dtype(dtype).itemsize


def _dot(a, b):
    return jnp.dot(a, b, preferred_element_type=F32)


def _rope_lanes(x, cos, sin_lo, sin_hi, half):
    return (x * cos + pltpu.roll(x, LANES - half, 1) * sin_lo
            + pltpu.roll(x, half, 1) * sin_hi)


def _rope_table_kernel(pos_ref, c_ref, cos_a, lo_a, hi_a, cos_b, lo_b, hi_b):
    pos = pos_ref[...]
    for base, (co, lo, hi) in ((0, (cos_a, lo_a, hi_a)), (3, (cos_b, lo_b, hi_b))):
        ang = pos * c_ref[base:base + 1, :]
        sn = jnp.sin(ang)
        co[...] = jnp.cos(ang)
        lo[...] = sn * c_ref[base + 1:base + 2, :]
        hi[...] = sn * c_ref[base + 2:base + 3, :]


def _lane_pattern(rot_dim):
    half = rot_dim // 2
    inv_freq = ROPE_THETA ** (-jnp.arange(half, dtype=F32) * 2.0 / rot_dim)
    lane = jnp.arange(LANES)
    freq = jnp.where(lane < rot_dim, jnp.tile(inv_freq, LANES // half), 0.0)
    lo = jnp.where(lane < half, -1.0, 0.0)
    hi = jnp.where((lane >= half) & (lane < rot_dim), 1.0, 0.0)
    return [freq.astype(F32), lo.astype(F32), hi.astype(F32)]


def _rope_tables(positions):
    m = positions.size
    pos = jnp.broadcast_to(positions.reshape(m, 1).astype(F32), (m, LANES))
    consts = jnp.stack(_lane_pattern(MLA_ROPE) + _lane_pattern(DIL_ROT)
                       + [jnp.zeros((LANES,), F32)] * 2)
    tm = 1024
    spec = pl.BlockSpec((tm, LANES), lambda i: (i, 0))
    out = jax.ShapeDtypeStruct((m, LANES), F32)
    return pl.pallas_call(
        _rope_table_kernel, grid=(m // tm,),
        in_specs=[spec, pl.BlockSpec((8, LANES), lambda i: (0, 0))],
        out_specs=[spec] * 6, out_shape=[out] * 6, name="rope_tables",
        compiler_params=_params(("parallel",), 7 * _nbytes((tm, LANES), F32)),
    )(pos, consts)


def _mm_kernel(a_ref, w_ref, o_ref):
    o_ref[...] = _dot(a_ref[...], w_ref[...]).astype(o_ref.dtype)


def _mm(a, w, out_dtype, tm, tn, name):
    m, k = a.shape
    n = w.shape[1]
    blk = (_nbytes((tm, k), a.dtype) + _nbytes((k, tn), w.dtype) + _nbytes((tm, tn), out_dtype))
    return pl.pallas_call(
        _mm_kernel, grid=(n // tn, m // tm),
        in_specs=[pl.BlockSpec((tm, k), lambda j, i: (i, 0)),
                  pl.BlockSpec((k, tn), lambda j, i: (0, j))],
        out_specs=pl.BlockSpec((tm, tn), lambda j, i: (i, j)),
        out_shape=jax.ShapeDtypeStruct((m, n), out_dtype), name=name,
        compiler_params=_params(("parallel", "parallel"), blk),
    )(a, w)


def _dil_proj_kernel(a_ref, w_ref, cos_ref, lo_ref, hi_ref, o_ref):
    acc = _dot(a_ref[...], w_ref[...])
    is_v = lax.rem(pl.program_id(0), 3) == 2

    @pl.when(is_v)
    def _():
        o_ref[...] = acc.astype(o_ref.dtype)

    @pl.when(jnp.logical_not(is_v))
    def _():
        cos, lo, hi = cos_ref[...], lo_ref[...], hi_ref[...]
        for h in range(DIL_HEADS):
            sl = slice(h * DIL_DIM, (h + 1) * DIL_DIM)
            o_ref[:, sl] = _rope_lanes(acc[:, sl], cos, lo, hi, DIL_ROT // 2).astype(o_ref.dtype)


def _dil_proj(x_bf, w_dil, tabs, tm):
    m, k = x_bf.shape
    n = w_dil.shape[1]
    tn = DIL_WIDTH
    tab_spec = pl.BlockSpec((tm, LANES), lambda j, i: (i, 0))
    blk = (_nbytes((tm, k), BF16) + _nbytes((k, tn), BF16) + _nbytes((tm, tn), BF16)
           + 3 * _nbytes((tm, LANES), F32))
    return pl.pallas_call(
        _dil_proj_kernel, grid=(n // tn, m // tm),
        in_specs=[pl.BlockSpec((tm, k), lambda j, i: (i, 0)),
                  pl.BlockSpec((k, tn), lambda j, i: (0, j)),
                  tab_spec, tab_spec, tab_spec],
        out_specs=pl.BlockSpec((tm, tn), lambda j, i: (i, j)),
        out_shape=jax.ShapeDtypeStruct((m, n), BF16), name="dil_proj",
        compiler_params=_params(("parallel", "parallel"), blk),
    )(x_bf, w_dil, *tabs)


def _latent_kernel(x_ref, wlat_ref, gq_ref, gkv_ref, wuq_ref, wukv_ref,
                   cos_ref, lo_ref, hi_ref, q_ref, k_ref, v_ref):
    lat = _dot(x_ref[...], wlat_ref[...])
    cos, lo, hi = cos_ref[...], lo_ref[...], hi_ref[...]

    def rms(c, g):
        return c * lax.rsqrt(jnp.mean(jnp.square(c), axis=-1, keepdims=True) + RMS_EPS) * g

    def rope(v):
        return _rope_lanes(v, cos, lo, hi, MLA_ROPE // 2)

    cq = rms(lat[:, :MLA_LORA], gq_ref[...]).astype(BF16)
    ckv = rms(lat[:, MLA_LORA:2 * MLA_LORA], gkv_ref[...]).astype(BF16)
    kpe = rope(lat[:, 2 * MLA_LORA:]).astype(BF16)
    q = _dot(cq, wuq_ref[...])
    kv = _dot(ckv, wukv_ref[...])
    for h in range(MLA_HEADS):
        b0 = h * 2 * LANES
        q_ref[:, b0:b0 + LANES] = q[:, b0:b0 + LANES].astype(BF16)
        q_ref[:, b0 + LANES:b0 + 2 * LANES] = rope(q[:, b0 + LANES:b0 + 2 * LANES]).astype(BF16)
        k_ref[:, b0:b0 + LANES] = kv[:, h * LANES:(h + 1) * LANES].astype(BF16)
        k_ref[:, b0 + LANES:b0 + 2 * LANES] = kpe
    v_ref[...] = kv[:, MLA_HEADS * MLA_NOPE:].astype(BF16)


def _latent(x_bf, w_lat, gq, gkv, w_uq, w_ukv, tabs, tm):
    m, k = x_bf.shape
    nl = w_lat.shape[1]
    hq = MLA_HEADS * 2 * LANES
    hv = MLA_HEADS * MLA_V
    const = lambda shape: pl.BlockSpec(shape, lambda i: (0, 0))
    row = lambda w: pl.BlockSpec((tm, w), lambda i: (i, 0))
    blk = (_nbytes((tm, k), BF16) + _nbytes((k, nl), BF16) + 2 * _nbytes((MLA_LORA, hq), BF16)
           + 2 * _nbytes((tm, hq), BF16) + _nbytes((tm, hv), BF16) + 3 * _nbytes((tm, LANES), F32))
    return pl.pallas_call(
        _latent_kernel, grid=(m // tm,),
        in_specs=[row(k), const((k, nl)), const((1, MLA_LORA)), const((1, MLA_LORA)),
                  const((MLA_LORA, hq)), const((MLA_LORA, hq)), row(LANES), row(LANES), row(LANES)],
        out_specs=[row(hq), row(hq), row(hv)],
        out_shape=[jax.ShapeDtypeStruct((m, hq), BF16), jax.ShapeDtypeStruct((m, hq), BF16),
                   jax.ShapeDtypeStruct((m, hv), BF16)],
        name="mla_latent",
        compiler_params=_params(("parallel",), blk),
    )(x_bf, w_lat, gq, gkv, w_uq, w_ukv, *tabs)


def _mla_attn_kernel(q_ref, k_ref, v_ref, o_ref, *, tq, scale):
    s = q_ref.shape[0]
    for qi in range(s // tq):
        s0, s1 = qi * tq, (qi + 1) * tq
        sc = lax.dot_general(q_ref[s0:s1, :], k_ref[:s1, :], (((1,), (1,)), ((), ())),
                             preferred_element_type=F32) * scale
        row = lax.broadcasted_iota(jnp.int32, (tq, s1), 0) + s0
        col = lax.broadcasted_iota(jnp.int32, (tq, s1), 1)
        sc = jnp.where(col <= row, sc, NEG_INF)
        mx = jnp.max(sc, axis=-1, keepdims=True)
        p = jnp.exp(sc - mx)
        den = jnp.sum(p, axis=-1, keepdims=True)
        o = _dot(p.astype(BF16), v_ref[:s1, :])
        o_ref[s0:s1, :] = (o / den).astype(o_ref.dtype)


def _mla_attention(q, k, v, batch, seq, tq=256):
    m = q.shape[0]
    dq = 2 * LANES
    blk = 2 * _nbytes((seq, dq), BF16) + 2 * _nbytes((seq, MLA_V), BF16)
    return pl.pallas_call(
        functools.partial(_mla_attn_kernel, tq=tq, scale=MLA_QK ** -0.5),
        grid=(batch, MLA_HEADS),
        in_specs=[pl.BlockSpec((seq, dq), lambda b, h: (b, h)),
                  pl.BlockSpec((seq, dq), lambda b, h: (b, h)),
                  pl.BlockSpec((seq, MLA_V), lambda b, h: (b, h))],
        out_specs=pl.BlockSpec((seq, MLA_V), lambda b, h: (b, h)),
        out_shape=jax.ShapeDtypeStruct((m, MLA_HEADS * MLA_V), BF16), name="mla_attention",
        compiler_params=_params(("parallel", "parallel"), blk + 4 * _nbytes((tq, seq), F32)),
    )(q, k, v)


def _dil_attn_kernel(q_ref, kc_ref, kp_ref, vc_ref, vp_ref, o_ref, lse_ref, *, nblk, scale):
    nb = DIL_BLOCK
    first = lax.rem(pl.program_id(0), nblk) == 0
    row = lax.broadcasted_iota(jnp.int32, (nb, 2 * nb), 0)
    col = lax.broadcasted_iota(jnp.int32, (nb, 2 * nb), 1)
    rel = row - col + nb
    valid = (rel >= 0) & (rel <= nb) & ((col >= nb) | jnp.logical_not(first))
    lane = lax.broadcasted_iota(jnp.int32, (nb, LANES), 1)
    lse_tile = jnp.zeros((nb, LANES), F32)
    for h in range(DIL_HEADS):
        sl = slice(h * DIL_DIM, (h + 1) * DIL_DIM)
        kk = jnp.concatenate([kp_ref[:, sl], kc_ref[:, sl]], axis=0)
        vv = jnp.concatenate([vp_ref[:, sl], vc_ref[:, sl]], axis=0)
        sc = lax.dot_general(q_ref[:, sl], kk, (((1,), (1,)), ((), ())),
                             preferred_element_type=F32) * scale
        sc = jnp.where(valid, sc, NEG_INF)
        mx = jnp.max(sc, axis=-1, keepdims=True)
        e = jnp.exp(sc - mx)
        den = jnp.sum(e, axis=-1, keepdims=True)
        p = (e / den).astype(BF16)
        o_ref[:, sl] = _dot(p, vv).astype(o_ref.dtype)
        lse_tile = jnp.where(lane == h, mx + jnp.log(den), lse_tile)
    lse_ref[...] = lse_tile


def _dil_attention(qkv, nblk):
    m = qkv.shape[0]
    nb = DIL_BLOCK
    w = DIL_WIDTH
    cur = lambda c: pl.BlockSpec((nb, w), lambda i: (i, c))
    prev = lambda c: pl.BlockSpec((nb, w), lambda i: (jnp.maximum(i - 1, 0), c))
    blk = 5 * _nbytes((nb, w), BF16) + _nbytes((nb, w), F32) + _nbytes((nb, LANES), F32)
    return pl.pallas_call(
        functools.partial(_dil_attn_kernel, nblk=nblk, scale=DIL_DIM ** -0.5),
        grid=(m // nb,),
        in_specs=[cur(0), cur(1), prev(1), cur(2), prev(2)],
        out_specs=[pl.BlockSpec((nb, w), lambda i: (i, 0)), pl.BlockSpec((nb, LANES), lambda i: (i, 0))],
        out_shape=[jax.ShapeDtypeStruct((m, w), F32), jax.ShapeDtypeStruct((m, LANES), F32)],
        name="dil_attention",
        compiler_params=_params(("parallel",), blk),
    )(qkv, qkv, qkv, qkv, qkv)


def _dil_combine_kernel(o1, o2, o3, l1, l2, l3, out_ref):
    a, b, c = l1[...], l2[...], l3[...]
    mx = jnp.maximum(jnp.maximum(a, b), c)
    ea, eb, ec = jnp.exp(a - mx), jnp.exp(b - mx), jnp.exp(c - mx)
    den = ea + eb + ec
    wa, wb, wc = ea / den, eb / den, ec / den
    for h in range(DIL_HEADS):
        sl = slice(h * DIL_DIM, (h + 1) * DIL_DIM)
        out_ref[:, sl] = (wa[:, h:h + 1] * o1[:, sl] + wb[:, h:h + 1] * o2[:, sl]
                          + wc[:, h:h + 1] * o3[:, sl]).astype(out_ref.dtype)


def _dil_combine(outs, lses, tm=512):
    m, w = outs[0].shape
    ospec = pl.BlockSpec((tm, w), lambda i: (i, 0))
    lspec = pl.BlockSpec((tm, LANES), lambda i: (i, 0))
    blk = 3 * _nbytes((tm, w), F32) + 3 * _nbytes((tm, LANES), F32) + _nbytes((tm, w), BF16)
    return pl.pallas_call(
        _dil_combine_kernel, grid=(m // tm,),
        in_specs=[ospec] * 3 + [lspec] * 3, out_specs=ospec,
        out_shape=jax.ShapeDtypeStruct((m, w), BF16), name="dil_combine",
        compiler_params=_params(("parallel",), blk),
    )(*outs, *lses)


POOL_HALO = 16


def _pool_kernel(cur_ref, halo_ref, w_ref, scale_ref, o_ref, buf, *, ts):
    i = pl.program_id(1)
    buf[0:POOL_HALO, :] = jnp.where(i > 0, halo_ref[...], 0.0)
    buf[POOL_HALO:, :] = cur_ref[...]
    t = i * ts + lax.broadcasted_iota(jnp.int32, (ts, 1), 0)
    for g, win in enumerate(POOL_WINDOWS):
        sl = slice(g * POOL_GROUP, (g + 1) * POOL_GROUP)
        tok = buf[POOL_HALO:POOL_HALO + ts, sl]
        acc = tok
        for j in range(1, win):
            acc = acc + buf[POOL_HALO - j:POOL_HALO - j + ts, sl]
        cnt = jnp.minimum(t + 1, win).astype(F32)
        d = (acc / cnt - tok).astype(BF16)
        o_ref[:, sl] = (_dot(d, w_ref[g]) * scale_ref[:, sl]).astype(o_ref.dtype)


def _pool_mixer(u_pool, w_grp, scale, batch, seq, ts=256):
    m, c = u_pool.shape
    nt = seq // ts
    hb = ts // POOL_HALO
    blk = (_nbytes((ts, c), F32) + _nbytes((POOL_HALO, c), F32) + _nbytes(w_grp.shape, BF16)
           + _nbytes((ts, c), BF16) + _nbytes((ts + POOL_HALO, c), F32))
    return pl.pallas_call(
        functools.partial(_pool_kernel, ts=ts), grid=(batch, nt),
        in_specs=[pl.BlockSpec((ts, c), lambda b, i: (b * nt + i, 0)),
                  pl.BlockSpec((POOL_HALO, c), lambda b, i: ((b * nt + i) * hb - jnp.minimum(i, 1), 0)),
                  pl.BlockSpec(w_grp.shape, lambda b, i: (0, 0, 0)),
                  pl.BlockSpec((1, c), lambda b, i: (0, 0))],
        out_specs=pl.BlockSpec((ts, c), lambda b, i: (b * nt + i, 0)),
        out_shape=jax.ShapeDtypeStruct((m, c), BF16),
        scratch_shapes=[pltpu.VMEM((ts + POOL_HALO, c), F32)], name="pool_mixer",
        compiler_params=_params(("parallel", "parallel"), blk),
    )(u_pool, u_pool, w_grp, scale)


CONV_HALO = 32


def _conv_kernel(cur_ref, halo_ref, w_ref, b_ref, g_ref, beta_ref, o_ref, buf, hbuf, *, ts):
    i = pl.program_id(1)
    c = CONV_WIDTH

    def glu(u):
        return u[:, :c] * jax.nn.sigmoid(u[:, c:])

    buf[0:CONV_HALO, :] = jnp.where(i > 0, glu(halo_ref[...]), 0.0)
    buf[CONV_HALO:, :] = glu(cur_ref[...])
    off = CONV_HALO - (CONV_K - 1)
    for cc in range(c // LANES):
        sl = slice(cc * LANES, (cc + 1) * LANES)
        acc = jnp.zeros((ts, LANES), F32)
        for k in range(CONV_K):
            acc = acc + w_ref[k:k + 1, sl] * buf[off + k:off + k + ts, sl]
        hbuf[:, sl] = acc + b_ref[:, sl]
    h = hbuf[...]
    mu = jnp.mean(h, axis=-1, keepdims=True)
    var = jnp.mean(jnp.square(h - mu), axis=-1, keepdims=True)
    y = (h - mu) * lax.rsqrt(var + LN_EPS) * g_ref[...] + beta_ref[...]
    o_ref[...] = (y * jax.nn.sigmoid(y)).astype(o_ref.dtype)


def _conv_mixer(u_conv, w_dw, b_dw, ln_g, ln_b, batch, seq, ts=256):
    m, c2 = u_conv.shape
    c = c2 // 2
    nt = seq // ts
    hb = ts // CONV_HALO
    vec = pl.BlockSpec((1, c), lambda b, i: (0, 0))
    blk = (_nbytes((ts, c2), F32) + _nbytes((CONV_HALO, c2), F32) + _nbytes(w_dw.shape, F32)
           + _nbytes((ts, c), BF16) + 2 * _nbytes((ts + CONV_HALO, c), F32))
    return pl.pallas_call(
        functools.partial(_conv_kernel, ts=ts), grid=(batch, nt),
        in_specs=[pl.BlockSpec((ts, c2), lambda b, i: (b * nt + i, 0)),
                  pl.BlockSpec((CONV_HALO, c2), lambda b, i: ((b * nt + i) * hb - jnp.minimum(i, 1), 0)),
                  pl.BlockSpec(w_dw.shape, lambda b, i: (0, 0)), vec, vec, vec],
        out_specs=pl.BlockSpec((ts, c), lambda b, i: (b * nt + i, 0)),
        out_shape=jax.ShapeDtypeStruct((m, c), BF16),
        scratch_shapes=[pltpu.VMEM((ts + CONV_HALO, c), F32), pltpu.VMEM((ts, c), F32)],
        name="conv_mixer",
        compiler_params=_params(("parallel", "parallel"), blk),
    )(u_conv, u_conv, w_dw, b_dw, ln_g, ln_b)


def _merge_kernel(x_ref, ha, hb, hc, hd, g0, g1, g2, g3, pa, pb, pc, pd, bias_ref, o_ref):
    x = x_ref[...]
    acc = None
    for b, (h_ref, wg_ref, wp_ref) in enumerate(((ha, g0, pa), (hb, g1, pb), (hc, g2, pc), (hd, g3, pd))):
        gate = jax.nn.sigmoid(_dot(x, wg_ref[...]) + bias_ref[b:b + 1, :])
        term = gate * _dot(h_ref[...], wp_ref[...])
        acc = term if acc is None else acc + term
    o_ref[...] = acc.astype(o_ref.dtype)


def _merge(x_bf, hs, w_gate, w_projs, b_gate, tm=512, tn=512):
    m, d = x_bf.shape
    kb = hs[0].shape[1]
    nj = d // tn
    gate_spec = lambda b: pl.BlockSpec((d, tn), lambda j, i: (0, b * nj + j))
    blk = (_nbytes((tm, d), BF16) + 4 * _nbytes((tm, kb), BF16) + 4 * _nbytes((d, tn), BF16)
           + 4 * _nbytes((kb, tn), BF16) + _nbytes((tm, tn), BF16) + 4 * _nbytes((tm, tn), F32))
    return pl.pallas_call(
        _merge_kernel, grid=(nj, m // tm),
        in_specs=([pl.BlockSpec((tm, d), lambda j, i: (i, 0))]
                  + [pl.BlockSpec((tm, kb), lambda j, i: (i, 0))] * 4
                  + [gate_spec(b) for b in range(N_BRANCHES)]
                  + [pl.BlockSpec((kb, tn), lambda j, i: (0, j))] * 4
                  + [pl.BlockSpec((N_BRANCHES, tn), lambda j, i: (0, j))]),
        out_specs=pl.BlockSpec((tm, tn), lambda j, i: (i, j)),
        out_shape=jax.ShapeDtypeStruct((m, d), BF16), name="gated_merge",
        compiler_params=_params(("parallel", "parallel"), blk),
    )(x_bf, *hs, w_gate, w_gate, w_gate, w_gate, *w_projs, b_gate)


def _proj_ln_kernel(a_ref, w_ref, res_ref, g_ref, b_ref, of_ref, ob_ref, acc_ref, *, alpha):
    k = pl.program_id(1)

    @pl.when(k == 0)
    def _():
        acc_ref[...] = jnp.zeros_like(acc_ref)

    acc_ref[...] += _dot(a_ref[...], w_ref[...])

    @pl.when(k == pl.num_programs(1) - 1)
    def _():
        y = alpha * res_ref[...] + acc_ref[...]
        mu = jnp.mean(y, axis=-1, keepdims=True)
        var = jnp.mean(jnp.square(y - mu), axis=-1, keepdims=True)
        out = (y - mu) * lax.rsqrt(var + LN_EPS) * g_ref[...] + b_ref[...]
        of_ref[...] = out
        ob_ref[...] = out.astype(BF16)


def _proj_ln(a, w, res, g, b, alpha, tm, tk, name):
    m, k = a.shape
    d = w.shape[1]
    vec = pl.BlockSpec((1, d), lambda i, kk: (0, 0))
    row = pl.BlockSpec((tm, d), lambda i, kk: (i, 0))
    blk = (_nbytes((tm, tk), BF16) + _nbytes((tk, d), BF16) + 2 * _nbytes((tm, d), F32)
           + _nbytes((tm, d), BF16) + _nbytes((tm, d), F32))
    return pl.pallas_call(
        functools.partial(_proj_ln_kernel, alpha=alpha), grid=(m // tm, k // tk),
        in_specs=[pl.BlockSpec((tm, tk), lambda i, kk: (i, kk)),
                  pl.BlockSpec((tk, d), lambda i, kk: (kk, 0)), row, vec, vec],
        out_specs=[row, row],
        out_shape=[jax.ShapeDtypeStruct((m, d), F32), jax.ShapeDtypeStruct((m, d), BF16)],
        scratch_shapes=[pltpu.VMEM((tm, d), F32)], name=name,
        compiler_params=_params(("parallel", "arbitrary"), blk),
    )(a, w, res, g, b)


FFN_HALO = 8


def _ffn_up_kernel(x_ref, wv_ref, wg_ref, dwv_ref, dwg_ref, bv_ref, bg_ref, o_ref, bufv, bufg,
                   *, tm, tiles_per_seq):
    first = lax.rem(pl.program_id(1), tiles_per_seq) == 0

    @pl.when(first)
    def _():
        bufv[0:FFN_HALO, :] = jnp.zeros((FFN_HALO, bufv.shape[1]), F32)
        bufg[0:FFN_HALO, :] = jnp.zeros((FFN_HALO, bufg.shape[1]), F32)

    @pl.when(jnp.logical_not(first))
    def _():
        bufv[0:FFN_HALO, :] = bufv[tm:tm + FFN_HALO, :]
        bufg[0:FFN_HALO, :] = bufg[tm:tm + FFN_HALO, :]

    x = x_ref[...]
    bufv[FFN_HALO:, :] = _dot(x, wv_ref[...])
    bufg[FFN_HALO:, :] = _dot(x, wg_ref[...])

    def conv(buf, w_ref, b_ref):
        off = FFN_HALO - (FFN_K - 1)
        acc = w_ref[0:1, :] * buf[off:off + tm, :]
        for k in range(1, FFN_K):
            acc = acc + w_ref[k:k + 1, :] * buf[off + k:off + k + tm, :]
        return acc + b_ref[...]

    val = conv(bufv, dwv_ref, bv_ref)
    gate = conv(bufg, dwg_ref, bg_ref)
    o_ref[...] = (val * (gate * jax.nn.sigmoid(gate))).astype(o_ref.dtype)


def _ffn_up(x_bf, w_up, dw, dw_b, seq, tm=512, tn=512):
    m, d = x_bf.shape
    f = w_up.shape[1] // 2
    nj = f // tn
    blk = (_nbytes((tm, d), BF16) + 2 * _nbytes((d, tn), BF16) + _nbytes((tm, tn), BF16)
           + 2 * _nbytes((tm + FFN_HALO, tn), F32))
    return pl.pallas_call(
        functools.partial(_ffn_up_kernel, tm=tm, tiles_per_seq=seq // tm), grid=(nj, m // tm),
        in_specs=[pl.BlockSpec((tm, d), lambda j, i: (i, 0)),
                  pl.BlockSpec((d, tn), lambda j, i: (0, j)),
                  pl.BlockSpec((d, tn), lambda j, i: (0, nj + j)),
                  pl.BlockSpec((FFN_K, tn), lambda j, i: (0, j)),
                  pl.BlockSpec((FFN_K, tn), lambda j, i: (0, nj + j)),
                  pl.BlockSpec((1, tn), lambda j, i: (0, j)),
                  pl.BlockSpec((1, tn), lambda j, i: (0, nj + j))],
        out_specs=pl.BlockSpec((tm, tn), lambda j, i: (i, j)),
        out_shape=jax.ShapeDtypeStruct((m, f), BF16),
        scratch_shapes=[pltpu.VMEM((tm + FFN_HALO, tn), F32)] * 2, name="ffn_up",
        compiler_params=_params(("parallel", "arbitrary"), blk),
    )(x_bf, w_up, w_up, dw, dw, dw_b, dw_b)


def _to_class_major(t, batch, seq, dil):
    if dil == 1:
        return t
    c = t.shape[-1]
    return t.reshape(batch, seq // dil, dil, c).transpose(0, 2, 1, 3).reshape(batch * seq, c)


def _from_class_major(t, batch, seq, dil):
    if dil == 1:
        return t
    c = t.shape[-1]
    return t.reshape(batch, dil, seq // dil, c).transpose(0, 2, 1, 3).reshape(batch * seq, c)


def _layer_weights(w_in, mla_w_uq, mla_w_ukv):
    d = w_in.shape[0]
    o_lat = 2 * MLA_LORA + MLA_ROPE
    o_pool = o_lat + POOL_WIDTH
    o_conv = o_pool + 2 * CONV_WIDTH
    o_dil = o_conv + len(DIL_CONFIGS) * 3 * DIL_WIDTH
    w_lat = jnp.pad(w_in[:, :o_lat], ((0, 0), (0, LANES - MLA_ROPE))).astype(BF16)
    w_pool = w_in[:, o_lat:o_pool].astype(BF16)
    w_conv = w_in[:, o_pool:o_conv].astype(BF16)
    w_dil = w_in[:, o_conv:o_dil].astype(BF16)
    w_gate = w_in[:, o_dil:].astype(BF16)
    w_uq = jnp.pad(mla_w_uq.reshape(MLA_LORA, MLA_HEADS, MLA_QK),
                   ((0, 0), (0, 0), (0, 2 * LANES - MLA_QK))).reshape(MLA_LORA, -1).astype(BF16)
    ukv = mla_w_ukv.reshape(MLA_LORA, MLA_HEADS, MLA_NOPE + MLA_V)
    w_ukv = jnp.concatenate([ukv[:, :, :MLA_NOPE].reshape(MLA_LORA, -1),
                             ukv[:, :, MLA_NOPE:].reshape(MLA_LORA, -1)], axis=1).astype(BF16)
    del d
    return w_lat, w_pool, w_conv, w_dil, w_gate, w_uq, w_ukv


def kernel(x, positions, w_in, b_gate, mla_gq, mla_gkv, mla_w_uq, mla_w_ukv, mla_w_proj, pool_w, pool_scale, pool_w_proj, conv_dw, conv_dw_b, conv_ln_g, conv_ln_b, conv_w_proj, dil_w_proj, mix_w_out, ln1_g, ln1_b, ffn_w_up, ffn_dw, ffn_dw_b, ffn_w_down, ln2_g, ln2_b):
    batch, seq, d_model = x.shape
    depth = w_in.shape[0]
    m = batch * seq
    alpha = (2 * depth) ** 0.25

    tabs = _rope_tables(positions)
    mla_tabs, dil_tabs = tabs[:3], tabs[3:]

    xf = x.reshape(m, d_model)
    xb = xf.astype(BF16)
    for l in range(depth):
        w_lat, w_pool, w_conv, w_dil, w_gate, w_uq, w_ukv = _layer_weights(
            w_in[l], mla_w_uq[l], mla_w_ukv[l])

        q, k, v = _latent(xb, w_lat, mla_gq[l][None], mla_gkv[l][None], w_uq, w_ukv, mla_tabs, tm=512)
        h_a = _mla_attention(q, k, v, batch, seq)

        u_pool = _mm(xb, w_pool, F32, 1024, POOL_WIDTH, "pool_in")
        h_b = _pool_mixer(u_pool, pool_w[l].astype(BF16), pool_scale[l][None], batch, seq)

        u_conv = _mm(xb, w_conv, F32, 1024, 1024, "conv_in")
        h_c = _conv_mixer(u_conv, jnp.pad(conv_dw[l], ((0, 1), (0, 0))), conv_dw_b[l][None],
                          conv_ln_g[l][None], conv_ln_b[l][None], batch, seq)

        qkv = _dil_proj(xb, w_dil, dil_tabs, tm=1024)
        outs, lses = [], []
        for g, (window, dil) in enumerate(DIL_CONFIGS):
            grp = _to_class_major(qkv[:, g * 3 * DIL_WIDTH:(g + 1) * 3 * DIL_WIDTH], batch, seq, dil)
            o_g, l_g = _dil_attention(grp, (seq // dil) // DIL_BLOCK)
            outs.append(_from_class_major(o_g, batch, seq, dil))
            lses.append(_from_class_major(l_g, batch, seq, dil))
        h_d = _dil_combine(outs, lses)

        merged = _merge(xb, (h_a, h_b, h_c, h_d), w_gate,
                        (mla_w_proj[l].astype(BF16), pool_w_proj[l].astype(BF16),
                         conv_w_proj[l].astype(BF16), dil_w_proj[l].astype(BF16)), b_gate[l])
        xf, xb = _proj_ln(merged, mix_w_out[l].astype(BF16), xf, ln1_g[l][None], ln1_b[l][None],
                          alpha, tm=512, tk=d_model, name="mix_out_ln1")

        hid = _ffn_up(xb, ffn_w_up[l].astype(BF16), ffn_dw[l], ffn_dw_b[l][None], seq)
        xf, xb = _proj_ln(hid, ffn_w_down[l].astype(BF16), xf, ln2_g[l][None], ln2_b[l][None],
                          alpha, tm=512, tk=512, name="ffn_down_ln2")
    return xf.reshape(batch, seq, d_model)
```

```python
import functools

import jax
import jax.numpy as jnp
from jax import lax
from jax.experimental import pallas as pl
from jax.experimental.pallas import tpu as pltpu

MLA_HEADS = 8
MLA_LORA = 512
MLA_NOPE = 128
MLA_ROPE = 64
MLA_V = 128
MLA_QK = MLA_NOPE + MLA_ROPE
POOL_WINDOWS = (2, 4, 8, 16)
POOL_GROUP = 256
POOL_WIDTH = 1024
CONV_WIDTH = 1024
CONV_K = 31
DIL_CONFIGS = ((128, 1), (512, 4), (2048, 16))
DIL_HEADS = 8
DIL_DIM = 128
DIL_WIDTH = DIL_HEADS * DIL_DIM
DIL_ROT = DIL_DIM // 4
DIL_BLOCK = 128
N_BRANCHES = 4
ROPE_THETA = 500000.0
FFN_K = 3
LN_EPS = 1e-5
RMS_EPS = 1e-6
NEG_INF = -1e30

OFF_LAT = 0
OFF_POOL = 2 * MLA_LORA
OFF_CONV = OFF_POOL + POOL_WIDTH
OFF_DIL = OFF_CONV + 2 * CONV_WIDTH
OFF_GATE = OFF_DIL + len(DIL_CONFIGS) * 3 * DIL_WIDTH

LANES = 128
V7X_VMEM_BYTES = 64 * 1024 * 1024
VMEM_CAP = V7X_VMEM_BYTES - 8 * 1024 * 1024

BF16 = jnp.bfloat16
F32 = jnp.float32


def _params(semantics, block_bytes):
    limit = min(VMEM_CAP, max(32 * 1024 * 1024, 2 * block_bytes + 16 * 1024 * 1024))
    return pltpu.CompilerParams(dimension_semantics=semantics, vmem_limit_bytes=limit)


def _nbytes(shape, dtype):
    n = 1
    for s in shape:
        n *= s
    return n * jnp.dtype(dtype).itemsize


def _dot(a, b):
    return jnp.dot(a, b, preferred_element_type=F32)


def _dot_nt(a, b):
    return lax.dot_general(a, b, (((1,), (1,)), ((), ())), preferred_element_type=F32)


def _layer_spec(block, layer, index_map):
    return pl.BlockSpec((None,) + tuple(block), lambda *g: (layer,) + tuple(index_map(*g)))


def _rope_lanes(x, cos, sin_lo, sin_hi, half):
    return (x * cos + pltpu.roll(x, LANES - half, 1) * sin_lo
            + pltpu.roll(x, half, 1) * sin_hi)


def _rope_table_kernel(pos_ref, c_ref, cos_a, lo_a, hi_a, cos_b, lo_b, hi_b):
    pos = pos_ref[...]
    for base, (co, lo, hi) in ((0, (cos_a, lo_a, hi_a)), (3, (cos_b, lo_b, hi_b))):
        ang = pos * c_ref[base:base + 1, :]
        sn = jnp.sin(ang)
        co[...] = jnp.cos(ang)
        lo[...] = sn * c_ref[base + 1:base + 2, :]
        hi[...] = sn * c_ref[base + 2:base + 3, :]


def _lane_pattern(rot_dim):
    half = rot_dim // 2
    inv_freq = ROPE_THETA ** (-jnp.arange(half, dtype=F32) * 2.0 / rot_dim)
    lane = jnp.arange(LANES)
    freq = jnp.where(lane < rot_dim, jnp.tile(inv_freq, LANES // half), 0.0)
    lo = jnp.where(lane < half, -1.0, 0.0)
    hi = jnp.where((lane >= half) & (lane < rot_dim), 1.0, 0.0)
    return [freq.astype(F32), lo.astype(F32), hi.astype(F32)]


def _rope_tables(positions):
    m = positions.size
    pos = jnp.broadcast_to(positions.reshape(m, 1).astype(F32), (m, LANES))
    consts = jnp.stack(_lane_pattern(MLA_ROPE) + _lane_pattern(DIL_ROT)
                       + [jnp.zeros((LANES,), F32)] * 2)
    tm = 1024
    spec = pl.BlockSpec((tm, LANES), lambda i: (i, 0))
    out = jax.ShapeDtypeStruct((m, LANES), F32)
    return pl.pallas_call(
        _rope_table_kernel, grid=(m // tm,),
        in_specs=[spec, pl.BlockSpec((8, LANES), lambda i: (0, 0))],
        out_specs=[spec] * 6, out_shape=[out] * 6, name="rope_tables",
        compiler_params=_params(("parallel",), 7 * _nbytes((tm, LANES), F32)),
    )(pos, consts)


def _mm_kernel(a_ref, w_ref, o_ref):
    o_ref[...] = _dot(a_ref[...], w_ref[...]).astype(o_ref.dtype)


def _in_proj(a, w_in, layer, col_off, n, out_dtype, tm, tn, name):
    m, k = a.shape
    c0 = col_off // tn
    blk = (_nbytes((tm, k), a.dtype) + _nbytes((k, tn), w_in.dtype) + _nbytes((tm, tn), out_dtype))
    return pl.pallas_call(
        _mm_kernel, grid=(n // tn, m // tm),
        in_specs=[pl.BlockSpec((tm, k), lambda j, i: (i, 0)),
                  _layer_spec((k, tn), layer, lambda j, i: (0, c0 + j))],
        out_specs=pl.BlockSpec((tm, tn), lambda j, i: (i, j)),
        out_shape=jax.ShapeDtypeStruct((m, n), out_dtype), name=name,
        compiler_params=_params(("parallel", "parallel"), blk),
    )(a, w_in)


def _dil_proj_kernel(a_ref, w_ref, cos_ref, lo_ref, hi_ref, o_ref):
    acc = _dot(a_ref[...], w_ref[...])
    is_v = lax.rem(pl.program_id(0), 3) == 2

    @pl.when(is_v)
    def _():
        o_ref[...] = acc.astype(o_ref.dtype)

    @pl.when(jnp.logical_not(is_v))
    def _():
        cos, lo, hi = cos_ref[...], lo_ref[...], hi_ref[...]
        for h in range(DIL_HEADS):
            sl = slice(h * DIL_DIM, (h + 1) * DIL_DIM)
            o_ref[:, sl] = _rope_lanes(acc[:, sl], cos, lo, hi, DIL_ROT // 2).astype(o_ref.dtype)


def _dil_proj(x_bf, w_in, layer, tabs, tm):
    m, k = x_bf.shape
    n = len(DIL_CONFIGS) * 3 * DIL_WIDTH
    tn = DIL_WIDTH
    c0 = OFF_DIL // tn
    tab_spec = pl.BlockSpec((tm, LANES), lambda j, i: (i, 0))
    blk = (_nbytes((tm, k), BF16) + _nbytes((k, tn), BF16) + _nbytes((tm, tn), BF16)
           + 3 * _nbytes((tm, LANES), F32))
    return pl.pallas_call(
        _dil_proj_kernel, grid=(n // tn, m // tm),
        in_specs=[pl.BlockSpec((tm, k), lambda j, i: (i, 0)),
                  _layer_spec((k, tn), layer, lambda j, i: (0, c0 + j)),
                  tab_spec, tab_spec, tab_spec],
        out_specs=pl.BlockSpec((tm, tn), lambda j, i: (i, j)),
        out_shape=jax.ShapeDtypeStruct((m, n), BF16), name="dil_proj",
        compiler_params=_params(("parallel", "parallel"), blk),
    )(x_bf, w_in, *tabs)


def _latent_kernel(x_ref, wlat_ref, wpe_ref, gq_ref, gkv_ref, wuq_ref, wukv_ref,
                   cos_ref, lo_ref, hi_ref, q_ref, k_ref, v_ref):
    x = x_ref[...]
    lat = _dot(x, wlat_ref[...])
    cos, lo, hi = cos_ref[...], lo_ref[...], hi_ref[...]

    def rms(c, g):
        return c * lax.rsqrt(jnp.mean(jnp.square(c), axis=-1, keepdims=True) + RMS_EPS) * g

    def rope(v):
        return _rope_lanes(v, cos, lo, hi, MLA_ROPE // 2)

    cq = rms(lat[:, :MLA_LORA], gq_ref[...]).astype(BF16)
    ckv = rms(lat[:, MLA_LORA:], gkv_ref[...]).astype(BF16)
    kpe = rope(_dot(x, wpe_ref[...])).astype(BF16)
    q = _dot(cq, wuq_ref[...])
    kv = _dot(ckv, wukv_ref[...])
    for h in range(MLA_HEADS):
        b0 = h * 2 * LANES
        q_ref[:, b0:b0 + LANES] = q[:, b0:b0 + LANES].astype(BF16)
        q_ref[:, b0 + LANES:b0 + 2 * LANES] = rope(q[:, b0 + LANES:b0 + 2 * LANES]).astype(BF16)
        k_ref[:, b0:b0 + LANES] = kv[:, h * LANES:(h + 1) * LANES].astype(BF16)
        k_ref[:, b0 + LANES:b0 + 2 * LANES] = kpe
    v_ref[...] = kv[:, MLA_HEADS * MLA_NOPE:].astype(BF16)


def _latent(x_bf, w_in, gq, gkv, w_uq, w_ukv, layer, tabs, tm):
    m, k = x_bf.shape
    nl = 2 * MLA_LORA
    hq = MLA_HEADS * 2 * LANES
    hv = MLA_HEADS * MLA_V
    pe_blk = (w_in.shape[2] - LANES) // LANES
    zero2 = lambda i: (0, 0)
    row = lambda w: pl.BlockSpec((tm, w), lambda i: (i, 0))
    blk = (_nbytes((tm, k), BF16) + _nbytes((k, nl + LANES), BF16) + 2 * _nbytes((MLA_LORA, hq), BF16)
           + 2 * _nbytes((tm, hq), BF16) + _nbytes((tm, hv), BF16) + 3 * _nbytes((tm, LANES), F32))
    return pl.pallas_call(
        _latent_kernel, grid=(m // tm,),
        in_specs=[row(k), _layer_spec((k, nl), layer, zero2),
                  _layer_spec((k, LANES), layer, lambda i: (0, pe_blk)),
                  _layer_spec((1, MLA_LORA), layer, zero2), _layer_spec((1, MLA_LORA), layer, zero2),
                  _layer_spec((MLA_LORA, hq), layer, zero2), _layer_spec((MLA_LORA, hq), layer, zero2),
                  row(LANES), row(LANES), row(LANES)],
        out_specs=[row(hq), row(hq), row(hv)],
        out_shape=[jax.ShapeDtypeStruct((m, hq), BF16), jax.ShapeDtypeStruct((m, hq), BF16),
                   jax.ShapeDtypeStruct((m, hv), BF16)],
        name="mla_latent",
        compiler_params=_params(("parallel",), blk),
    )(x_bf, w_in, w_in, gq, gkv, w_uq, w_ukv, *tabs)


def _mla_attn_kernel(q_ref, k_ref, v_ref, o_ref, *, tq, scale):
    s = q_ref.shape[0]
    for qi in range(s // tq):
        s0, s1 = qi * tq, (qi + 1) * tq
        sc = _dot_nt(q_ref[s0:s1, :], k_ref[:s1, :]) * scale
        row = lax.broadcasted_iota(jnp.int32, (tq, s1), 0) + s0
        col = lax.broadcasted_iota(jnp.int32, (tq, s1), 1)
        sc = jnp.where(col <= row, sc, NEG_INF)
        mx = jnp.max(sc, axis=-1, keepdims=True)
        p = jnp.exp(sc - mx)
        den = jnp.sum(p, axis=-1, keepdims=True)
        o = _dot(p.astype(BF16), v_ref[:s1, :])
        o_ref[s0:s1, :] = (o / den).astype(o_ref.dtype)


def _mla_attention(q, k, v, batch, seq, tq=256):
    m = q.shape[0]
    dq = 2 * LANES
    blk = 2 * _nbytes((seq, dq), BF16) + 2 * _nbytes((seq, MLA_V), BF16)
    return pl.pallas_call(
        functools.partial(_mla_attn_kernel, tq=tq, scale=MLA_QK ** -0.5),
        grid=(batch, MLA_HEADS),
        in_specs=[pl.BlockSpec((seq, dq), lambda b, h: (b, h)),
                  pl.BlockSpec((seq, dq), lambda b, h: (b, h)),
                  pl.BlockSpec((seq, MLA_V), lambda b, h: (b, h))],
        out_specs=pl.BlockSpec((seq, MLA_V), lambda b, h: (b, h)),
        out_shape=jax.ShapeDtypeStruct((m, MLA_HEADS * MLA_V), BF16), name="mla_attention",
        compiler_params=_params(("parallel", "parallel"), blk + 4 * _nbytes((tq, seq), F32)),
    )(q, k, v)


def _dil_attn_kernel(q1, k1, v1, q2, k2, v2, q3, k3, v3, o_ref, stage, o_acc, lse_acc, *, seq, scale):
    nb = DIL_BLOCK
    row = lax.broadcasted_iota(jnp.int32, (nb, 2 * nb), 0)
    col = lax.broadcasted_iota(jnp.int32, (nb, 2 * nb), 1)
    rel = row - col + nb
    band = (rel >= 0) & (rel <= nb)
    causal = (lax.broadcasted_iota(jnp.int32, (nb, nb), 1)
              <= lax.broadcasted_iota(jnp.int32, (nb, nb), 0))

    def attend(q, k, v, mask):
        sc = jnp.where(mask, _dot_nt(q, k) * scale, NEG_INF)
        mx = jnp.max(sc, axis=-1, keepdims=True)
        e = jnp.exp(sc - mx)
        den = jnp.sum(e, axis=-1, keepdims=True)
        o = _dot((e / den).astype(BF16), v)
        return o, jnp.broadcast_to(mx + jnp.log(den), (nb, LANES))

    o, l = attend(q1[0:nb, :], k1[0:nb, :], v1[0:nb, :], causal)
    o_acc[0, 0:nb, :] = o
    lse_acc[0, 0:nb, :] = l

    def block_body(blk, carry):
        q0 = pl.multiple_of(blk * nb, nb)
        k0 = pl.multiple_of(blk * nb - nb, nb)
        o, l = attend(q1[pl.ds(q0, nb), :], k1[pl.ds(k0, 2 * nb), :], v1[pl.ds(k0, 2 * nb), :], band)
        o_acc[0, pl.ds(q0, nb), :] = o
        lse_acc[0, pl.ds(q0, nb), :] = l
        return carry

    lax.fori_loop(1, seq // nb, block_body, 0)

    for gi, (q_ref, k_ref, v_ref) in ((1, (q2, k2, v2)), (2, (q3, k3, v3))):
        dil = DIL_CONFIGS[gi][1]
        length = seq // dil
        stage[0] = q_ref[...].astype(F32)
        stage[1] = k_ref[...].astype(F32)
        stage[2] = v_ref[...].astype(F32)

        def class_body(r, carry, gi=gi, dil=dil, length=length):
            qc = stage[0, pl.ds(r, length, stride=dil), :].astype(BF16)
            kc = stage[1, pl.ds(r, length, stride=dil), :].astype(BF16)
            vc = stage[2, pl.ds(r, length, stride=dil), :].astype(BF16)
            for blk in range(length // nb):
                q_blk = qc[blk * nb:(blk + 1) * nb]
                if blk == 0:
                    o, l = attend(q_blk, kc[0:nb], vc[0:nb], causal)
                else:
                    o, l = attend(q_blk, kc[(blk - 1) * nb:(blk + 1) * nb],
                                  vc[(blk - 1) * nb:(blk + 1) * nb], band)
                rows = pl.ds(r + blk * nb * dil, nb, stride=dil)
                o_acc[gi, rows, :] = o
                lse_acc[gi, rows, :] = l
            return carry

        lax.fori_loop(0, dil, class_body, 0)

    chunk = 2 * nb

    def merge_body(c, carry):
        rows = pl.ds(pl.multiple_of(c * chunk, chunk), chunk)
        la, lb, lc = lse_acc[0, rows, :], lse_acc[1, rows, :], lse_acc[2, rows, :]
        mx = jnp.maximum(jnp.maximum(la, lb), lc)
        ea, eb, ec = jnp.exp(la - mx), jnp.exp(lb - mx), jnp.exp(lc - mx)
        den = ea + eb + ec
        out = ((ea / den) * o_acc[0, rows, :] + (eb / den) * o_acc[1, rows, :]
               + (ec / den) * o_acc[2, rows, :])
        o_ref[rows, :] = out.astype(o_ref.dtype)
        return carry

    lax.fori_loop(0, seq // chunk, merge_body, 0)


def _dil_attention(qkv, batch, seq):
    m = qkv.shape[0]
    col_blocks = DIL_WIDTH // LANES
    specs = [pl.BlockSpec((seq, LANES), lambda b, h, c=(g * 3 + j) * col_blocks: (b, c + h))
             for g in range(len(DIL_CONFIGS)) for j in range(3)]
    blk = 10 * _nbytes((seq, LANES), BF16)
    scratch = 9 * _nbytes((seq, LANES), F32)
    return pl.pallas_call(
        functools.partial(_dil_attn_kernel, seq=seq, scale=DIL_DIM ** -0.5),
        grid=(batch, DIL_HEADS),
        in_specs=specs,
        out_specs=pl.BlockSpec((seq, LANES), lambda b, h: (b, h)),
        out_shape=jax.ShapeDtypeStruct((m, DIL_WIDTH), BF16),
        scratch_shapes=[pltpu.VMEM((3, seq, LANES), F32)] * 3,
        name="dil_attention",
        compiler_params=_params(("parallel", "parallel"), blk + scratch // 2),
    )(*([qkv] * 9))


POOL_HALO = 16


def _pool_kernel(cur_ref, halo_ref, w_ref, scale_ref, o_ref, buf, *, ts):
    i = pl.program_id(1)
    buf[0:POOL_HALO, :] = jnp.where(i > 0, halo_ref[...], 0.0)
    buf[POOL_HALO:, :] = cur_ref[...]
    t = i * ts + lax.broadcasted_iota(jnp.int32, (ts, 1), 0)
    for g, win in enumerate(POOL_WINDOWS):
        sl = slice(g * POOL_GROUP, (g + 1) * POOL_GROUP)
        tok = buf[POOL_HALO:POOL_HALO + ts, sl]
        acc = tok
        for j in range(1, win):
            acc = acc + buf[POOL_HALO - j:POOL_HALO - j + ts, sl]
        cnt = jnp.minimum(t + 1, win).astype(F32)
        d = (acc / cnt - tok).astype(BF16)
        o_ref[:, sl] = (_dot(d, w_ref[g]) * scale_ref[:, sl]).astype(o_ref.dtype)


def _pool_mixer(u_pool, w_grp, scale, layer, batch, seq, ts=256):
    m, c = u_pool.shape
    nt = seq // ts
    hb = ts // POOL_HALO
    blk = (_nbytes((ts, c), F32) + _nbytes((POOL_HALO, c), F32) + _nbytes(w_grp.shape[1:], BF16)
           + _nbytes((ts, c), BF16) + _nbytes((ts + POOL_HALO, c), F32))
    return pl.pallas_call(
        functools.partial(_pool_kernel, ts=ts), grid=(batch, nt),
        in_specs=[pl.BlockSpec((ts, c), lambda b, i: (b * nt + i, 0)),
                  pl.BlockSpec((POOL_HALO, c), lambda b, i: ((b * nt + i) * hb - jnp.minimum(i, 1), 0)),
                  _layer_spec(w_grp.shape[1:], layer, lambda b, i: (0, 0, 0)),
                  _layer_spec((1, c), layer, lambda b, i: (0, 0))],
        out_specs=pl.BlockSpec((ts, c), lambda b, i: (b * nt + i, 0)),
        out_shape=jax.ShapeDtypeStruct((m, c), BF16),
        scratch_shapes=[pltpu.VMEM((ts + POOL_HALO, c), F32)], name="pool_mixer",
        compiler_params=_params(("parallel", "parallel"), blk),
    )(u_pool, u_pool, w_grp, scale)


CONV_HALO = 32


def _conv_kernel(cur_ref, halo_ref, w_ref, b_ref, g_ref, beta_ref, o_ref, buf, hbuf, *, ts):
    i = pl.program_id(1)
    c = CONV_WIDTH

    def glu(u):
        return u[:, :c] * jax.nn.sigmoid(u[:, c:])

    buf[0:CONV_HALO, :] = jnp.where(i > 0, glu(halo_ref[...]), 0.0)
    buf[CONV_HALO:, :] = glu(cur_ref[...])
    off = CONV_HALO - (CONV_K - 1)
    for cc in range(c // LANES):
        sl = slice(cc * LANES, (cc + 1) * LANES)
        acc = jnp.zeros((ts, LANES), F32)
        for k in range(CONV_K):
            acc = acc + w_ref[k:k + 1, sl] * buf[off + k:off + k + ts, sl]
        hbuf[:, sl] = acc + b_ref[:, sl]
    h = hbuf[...]
    mu = jnp.mean(h, axis=-1, keepdims=True)
    var = jnp.mean(jnp.square(h - mu), axis=-1, keepdims=True)
    y = (h - mu) * lax.rsqrt(var + LN_EPS) * g_ref[...] + beta_ref[...]
    o_ref[...] = (y * jax.nn.sigmoid(y)).astype(o_ref.dtype)


def _conv_mixer(u_conv, w_dw, b_dw, ln_g, ln_b, layer, batch, seq, ts=256):
    m, c2 = u_conv.shape
    c = c2 // 2
    nt = seq // ts
    hb = ts // CONV_HALO
    vec = _layer_spec((1, c), layer, lambda b, i: (0, 0))
    blk = (_nbytes((ts, c2), F32) + _nbytes((CONV_HALO, c2), F32) + _nbytes(w_dw.shape[1:], F32)
           + _nbytes((ts, c), BF16) + 2 * _nbytes((ts + CONV_HALO, c), F32))
    return pl.pallas_call(
        functools.partial(_conv_kernel, ts=ts), grid=(batch, nt),
        in_specs=[pl.BlockSpec((ts, c2), lambda b, i: (b * nt + i, 0)),
                  pl.BlockSpec((CONV_HALO, c2), lambda b, i: ((b * nt + i) * hb - jnp.minimum(i, 1), 0)),
                  _layer_spec(w_dw.shape[1:], layer, lambda b, i: (0, 0)), vec, vec, vec],
        out_specs=pl.BlockSpec((ts, c), lambda b, i: (b * nt + i, 0)),
        out_shape=jax.ShapeDtypeStruct((m, c), BF16),
        scratch_shapes=[pltpu.VMEM((ts + CONV_HALO, c), F32), pltpu.VMEM((ts, c), F32)],
        name="conv_mixer",
        compiler_params=_params(("parallel", "parallel"), blk),
    )(u_conv, u_conv, w_dw, b_dw, ln_g, ln_b)


def _merge_kernel(x_ref, ha, hb, hc, hd, g0, g1, g2, g3, pa, pb, pc, pd, bias_ref, o_ref):
    x = x_ref[...]
    acc = None
    for b, (h_ref, wg_ref, wp_ref) in enumerate(((ha, g0, pa), (hb, g1, pb), (hc, g2, pc), (hd, g3, pd))):
        gate = jax.nn.sigmoid(_dot(x, wg_ref[...]) + bias_ref[b:b + 1, :])
        term = gate * _dot(h_ref[...], wp_ref[...])
        acc = term if acc is None else acc + term
    o_ref[...] = acc.astype(o_ref.dtype)


def _merge(x_bf, hs, w_in, w_projs, b_gate, layer, tm=512, tn=512):
    m, d = x_bf.shape
    kb = hs[0].shape[1]
    nj = d // tn
    c0 = OFF_GATE // tn
    gate_spec = lambda b: _layer_spec((d, tn), layer, lambda j, i: (0, c0 + b * nj + j))
    blk = (_nbytes((tm, d), BF16) + 4 * _nbytes((tm, kb), BF16) + 4 * _nbytes((d, tn), BF16)
           + 4 * _nbytes((kb, tn), BF16) + _nbytes((tm, tn), BF16) + 4 * _nbytes((tm, tn), F32))
    return pl.pallas_call(
        _merge_kernel, grid=(nj, m // tm),
        in_specs=([pl.BlockSpec((tm, d), lambda j, i: (i, 0))]
                  + [pl.BlockSpec((tm, kb), lambda j, i: (i, 0))] * 4
                  + [gate_spec(b) for b in range(N_BRANCHES)]
                  + [_layer_spec((kb, tn), layer, lambda j, i: (0, j))] * 4
                  + [_layer_spec((N_BRANCHES, tn), layer, lambda j, i: (0, j))]),
        out_specs=pl.BlockSpec((tm, tn), lambda j, i: (i, j)),
        out_shape=jax.ShapeDtypeStruct((m, d), BF16), name="gated_merge",
        compiler_params=_params(("parallel", "parallel"), blk),
    )(x_bf, *hs, w_in, w_in, w_in, w_in, *w_projs, b_gate)


def _proj_ln_kernel(a_ref, w_ref, res_ref, g_ref, b_ref, of_ref, ob_ref, acc_ref, *, alpha):
    k = pl.program_id(1)

    @pl.when(k == 0)
    def _():
        acc_ref[...] = jnp.zeros_like(acc_ref)

    acc_ref[...] += _dot(a_ref[...], w_ref[...])

    @pl.when(k == pl.num_programs(1) - 1)
    def _():
        y = alpha * res_ref[...] + acc_ref[...]
        mu = jnp.mean(y, axis=-1, keepdims=True)
        var = jnp.mean(jnp.square(y - mu), axis=-1, keepdims=True)
        out = (y - mu) * lax.rsqrt(var + LN_EPS) * g_ref[...] + b_ref[...]
        of_ref[...] = out
        ob_ref[...] = out.astype(BF16)


def _proj_ln(a, w, res, g, b, layer, alpha, tm, tk, name):
    m, k = a.shape
    d = w.shape[2]
    vec = _layer_spec((1, d), layer, lambda i, kk: (0, 0))
    row = pl.BlockSpec((tm, d), lambda i, kk: (i, 0))
    blk = (_nbytes((tm, tk), BF16) + _nbytes((tk, d), BF16) + 2 * _nbytes((tm, d), F32)
           + _nbytes((tm, d), BF16) + _nbytes((tm, d), F32))
    return pl.pallas_call(
        functools.partial(_proj_ln_kernel, alpha=alpha), grid=(m // tm, k // tk),
        in_specs=[pl.BlockSpec((tm, tk), lambda i, kk: (i, kk)),
                  _layer_spec((tk, d), layer, lambda i, kk: (kk, 0)), row, vec, vec],
        out_specs=[row, row],
        out_shape=[jax.ShapeDtypeStruct((m, d), F32), jax.ShapeDtypeStruct((m, d), BF16)],
        scratch_shapes=[pltpu.VMEM((tm, d), F32)], name=name,
        compiler_params=_params(("parallel", "arbitrary"), blk),
    )(a, w, res, g, b)


FFN_HALO = 8


def _ffn_up_kernel(x_ref, wv_ref, wg_ref, dwv_ref, dwg_ref, bv_ref, bg_ref, o_ref, bufv, bufg,
                   *, tm, tiles_per_seq):
    first = lax.rem(pl.program_id(1), tiles_per_seq) == 0

    @pl.when(first)
    def _():
        bufv[0:FFN_HALO, :] = jnp.zeros((FFN_HALO, bufv.shape[1]), F32)
        bufg[0:FFN_HALO, :] = jnp.zeros((FFN_HALO, bufg.shape[1]), F32)

    @pl.when(jnp.logical_not(first))
    def _():
        bufv[0:FFN_HALO, :] = bufv[tm:tm + FFN_HALO, :]
        bufg[0:FFN_HALO, :] = bufg[tm:tm + FFN_HALO, :]

    x = x_ref[...]
    bufv[FFN_HALO:, :] = _dot(x, wv_ref[...])
    bufg[FFN_HALO:, :] = _dot(x, wg_ref[...])

    def conv(buf, w_ref, b_ref):
        off = FFN_HALO - (FFN_K - 1)
        acc = w_ref[0:1, :] * buf[off:off + tm, :]
        for k in range(1, FFN_K):
            acc = acc + w_ref[k:k + 1, :] * buf[off + k:off + k + tm, :]
        return acc + b_ref[...]

    val = conv(bufv, dwv_ref, bv_ref)
    gate = conv(bufg, dwg_ref, bg_ref)
    o_ref[...] = (val * (gate * jax.nn.sigmoid(gate))).astype(o_ref.dtype)


def _ffn_up(x_bf, w_up, dw, dw_b, layer, seq, tm=512, tn=512):
    m, d = x_bf.shape
    f = w_up.shape[2] // 2
    nj = f // tn
    blk = (_nbytes((tm, d), BF16) + 2 * _nbytes((d, tn), BF16) + _nbytes((tm, tn), BF16)
           + 2 * _nbytes((tm + FFN_HALO, tn), F32))
    return pl.pallas_call(
        functools.partial(_ffn_up_kernel, tm=tm, tiles_per_seq=seq // tm), grid=(nj, m // tm),
        in_specs=[pl.BlockSpec((tm, d), lambda j, i: (i, 0)),
                  _layer_spec((d, tn), layer, lambda j, i: (0, j)),
                  _layer_spec((d, tn), layer, lambda j, i: (0, nj + j)),
                  _layer_spec((FFN_K, tn), layer, lambda j, i: (0, j)),
                  _layer_spec((FFN_K, tn), layer, lambda j, i: (0, nj + j)),
                  _layer_spec((1, tn), layer, lambda j, i: (0, j)),
                  _layer_spec((1, tn), layer, lambda j, i: (0, nj + j))],
        out_specs=pl.BlockSpec((tm, tn), lambda j, i: (i, j)),
        out_shape=jax.ShapeDtypeStruct((m, f), BF16),
        scratch_shapes=[pltpu.VMEM((tm + FFN_HALO, tn), F32)] * 2, name="ffn_up",
        compiler_params=_params(("parallel", "arbitrary"), blk),
    )(x_bf, w_up, w_up, dw, dw, dw_b, dw_b)


def _prep_in_proj(w_in):
    depth, d, _ = w_in.shape
    pe0 = 2 * MLA_LORA
    pe1 = pe0 + MLA_ROPE
    return jnp.concatenate(
        [w_in[:, :, :pe0].astype(BF16), w_in[:, :, pe1:].astype(BF16), w_in[:, :, pe0:pe1].astype(BF16),
         jnp.zeros((depth, d, LANES - MLA_ROPE), BF16)], axis=2)


def _prep_mla_up(mla_w_uq, mla_w_ukv):
    depth = mla_w_uq.shape[0]
    w_uq = jnp.pad(mla_w_uq.reshape(depth, MLA_LORA, MLA_HEADS, MLA_QK),
                   ((0, 0), (0, 0), (0, 0), (0, 2 * LANES - MLA_QK)))
    w_uq = w_uq.reshape(depth, MLA_LORA, -1).astype(BF16)
    ukv = mla_w_ukv.reshape(depth, MLA_LORA, MLA_HEADS, MLA_NOPE + MLA_V)
    w_ukv = jnp.concatenate([ukv[..., :MLA_NOPE].reshape(depth, MLA_LORA, -1),
                             ukv[..., MLA_NOPE:].reshape(depth, MLA_LORA, -1)], axis=2).astype(BF16)
    return w_uq, w_ukv


def kernel(x, positions, w_in, b_gate, mla_gq, mla_gkv, mla_w_uq, mla_w_ukv, mla_w_proj, pool_w, pool_scale, pool_w_proj, conv_dw, conv_dw_b, conv_ln_g, conv_ln_b, conv_w_proj, dil_w_proj, mix_w_out, ln1_g, ln1_b, ffn_w_up, ffn_dw, ffn_dw_b, ffn_w_down, ln2_g, ln2_b):
    batch, seq, d_model = x.shape
    depth = w_in.shape[0]
    m = batch * seq
    alpha = (2 * depth) ** 0.25
    ffn_dim = ffn_w_down.shape[1]

    tabs = _rope_tables(positions)
    mla_tabs, dil_tabs = tabs[:3], tabs[3:]

    w_in_r = _prep_in_proj(w_in)
    w_uq, w_ukv = _prep_mla_up(mla_w_uq, mla_w_ukv)
    w_projs = tuple(w.astype(BF16) for w in (mla_w_proj, pool_w_proj, conv_w_proj, dil_w_proj))
    w_out = mix_w_out.astype(BF16)
    w_up = ffn_w_up.astype(BF16)
    w_down = ffn_w_down.astype(BF16)
    pool_w_bf = pool_w.astype(BF16)
    conv_dw_p = jnp.pad(conv_dw, ((0, 0), (0, 1), (0, 0)))
    vec = lambda v: v[:, None, :]
    gq, gkv, p_scale = vec(mla_gq), vec(mla_gkv), vec(pool_scale)
    c_b, c_g, c_beta = vec(conv_dw_b), vec(conv_ln_g), vec(conv_ln_b)
    g1, b1, g2, b2 = vec(ln1_g), vec(ln1_b), vec(ln2_g), vec(ln2_b)
    f_b = vec(ffn_dw_b)

    xf = x.reshape(m, d_model)
    xb = xf.astype(BF16)
    for l in range(depth):
        q, k, v = _latent(xb, w_in_r, gq, gkv, w_uq, w_ukv, l, mla_tabs, tm=512)
        h_a = _mla_attention(q, k, v, batch, seq)

        u_pool = _in_proj(xb, w_in_r, l, OFF_POOL, POOL_WIDTH, F32, 1024, 1024, "pool_in")
        h_b = _pool_mixer(u_pool, pool_w_bf, p_scale, l, batch, seq)

        u_conv = _in_proj(xb, w_in_r, l, OFF_CONV, 2 * CONV_WIDTH, F32, 1024, 1024, "conv_in")
        h_c = _conv_mixer(u_conv, conv_dw_p, c_b, c_g, c_beta, l, batch, seq)

        qkv = _dil_proj(xb, w_in_r, l, dil_tabs, tm=1024)
        h_d = _dil_attention(qkv, batch, seq)

        merged = _merge(xb, (h_a, h_b, h_c, h_d), w_in_r, w_projs, b_gate, l)
        xf, xb = _proj_ln(merged, w_out, xf, g1, b1, l, alpha, tm=512, tk=d_model, name="mix_out_ln1")

        hid = _ffn_up(xb, w_up, ffn_dw, f_b, l, seq)
        xf, xb = _proj_ln(hid, w_down, xf, g2, b2, l, alpha, tm=512, tk=ffn_dim // 4, name="ffn_down_ln2")
    return xf.reshape(batch, seq, d_model)
```

```python
import functools

import jax
import jax.numpy as jnp
from jax import lax
from jax.experimental import pallas as pl
from jax.experimental.pallas import tpu as pltpu

MLA_HEADS = 8
MLA_LORA = 512
MLA_NOPE = 128
MLA_ROPE = 64
MLA_V = 128
MLA_QK = MLA_NOPE + MLA_ROPE
POOL_WINDOWS = (2, 4, 8, 16)
POOL_GROUP = 256
POOL_WIDTH = 1024
CONV_WIDTH = 1024
CONV_K = 31
DIL_CONFIGS = ((128, 1), (512, 4), (2048, 16))
DIL_HEADS = 8
DIL_DIM = 128
DIL_WIDTH = DIL_HEADS * DIL_DIM
DIL_ROT = DIL_DIM // 4
DIL_BLOCK = 128
N_BRANCHES = 4
ROPE_THETA = 500000.0
FFN_K = 3
LN_EPS = 1e-5
RMS_EPS = 1e-6
NEG_INF = -1e30

OFF_POOL = 0
OFF_CONV = OFF_POOL + POOL_WIDTH
OFF_DIL = OFF_CONV + 2 * CONV_WIDTH
OFF_GATE = OFF_DIL + len(DIL_CONFIGS) * 3 * DIL_WIDTH

LANES = 128
V7X_VMEM_BYTES = 64 * 1024 * 1024
VMEM_CAP = V7X_VMEM_BYTES - 8 * 1024 * 1024

BF16 = jnp.bfloat16
F32 = jnp.float32


def _params(semantics, block_bytes, flags=None):
    limit = min(VMEM_CAP, max(32 * 1024 * 1024, 2 * block_bytes + 16 * 1024 * 1024))
    return pltpu.CompilerParams(dimension_semantics=semantics, vmem_limit_bytes=limit, flags=flags)


def _nbytes(shape, dtype):
    n = 1
    for s in shape:
        n *= s
    return n * jnp.dtype(dtype).itemsize


def _dot(a, b):
    return jnp.dot(a, b, preferred_element_type=F32)


def _dot_nt(a, b):
    return lax.dot_general(a, b, (((1,), (1,)), ((), ())), preferred_element_type=F32)


def _layer_spec(block, layer, index_map):
    return pl.BlockSpec((None,) + tuple(block), lambda *g: (layer,) + tuple(index_map(*g)))


def _rope_lanes(x, cos, sin_lo, sin_hi, half):
    return (x * cos + pltpu.roll(x, LANES - half, 1) * sin_lo
            + pltpu.roll(x, half, 1) * sin_hi)


def _rope_table_kernel(pos_ref, c_ref, cos_a, lo_a, hi_a, cos_b, lo_b, hi_b):
    pos = pos_ref[...]
    for base, (co, lo, hi) in ((0, (cos_a, lo_a, hi_a)), (3, (cos_b, lo_b, hi_b))):
        ang = pos * c_ref[base:base + 1, :]
        sn = jnp.sin(ang)
        co[...] = jnp.cos(ang)
        lo[...] = sn * c_ref[base + 1:base + 2, :]
        hi[...] = sn * c_ref[base + 2:base + 3, :]


def _lane_pattern(rot_dim):
    half = rot_dim // 2
    inv_freq = ROPE_THETA ** (-jnp.arange(half, dtype=F32) * 2.0 / rot_dim)
    lane = jnp.arange(LANES)
    freq = jnp.where(lane < rot_dim, jnp.tile(inv_freq, LANES // half), 0.0)
    lo = jnp.where(lane < half, -1.0, 0.0)
    hi = jnp.where((lane >= half) & (lane < rot_dim), 1.0, 0.0)
    return [freq.astype(F32), lo.astype(F32), hi.astype(F32)]


def _rope_tables(positions):
    m = positions.size
    pos = jnp.broadcast_to(positions.reshape(m, 1).astype(F32), (m, LANES))
    consts = jnp.stack(_lane_pattern(MLA_ROPE) + _lane_pattern(DIL_ROT)
                       + [jnp.zeros((LANES,), F32)] * 2)
    tm = 1024
    spec = pl.BlockSpec((tm, LANES), lambda i: (i, 0))
    out = jax.ShapeDtypeStruct((m, LANES), F32)
    return pl.pallas_call(
        _rope_table_kernel, grid=(m // tm,),
        in_specs=[spec, pl.BlockSpec((8, LANES), lambda i: (0, 0))],
        out_specs=[spec] * 6, out_shape=[out] * 6, name="rope_tables",
        compiler_params=_params(("parallel",), 7 * _nbytes((tm, LANES), F32)),
    )(pos, consts)


def _mm_kernel(a_ref, w_ref, o_ref):
    o_ref[...] = _dot(a_ref[...], w_ref[...]).astype(o_ref.dtype)


def _in_proj(a, w_in, layer, col_off, n, out_dtype, tm, tn, name):
    m, k = a.shape
    c0 = col_off // tn
    blk = (_nbytes((tm, k), a.dtype) + _nbytes((k, tn), w_in.dtype) + _nbytes((tm, tn), out_dtype))
    return pl.pallas_call(
        _mm_kernel, grid=(n // tn, m // tm),
        in_specs=[pl.BlockSpec((tm, k), lambda j, i: (i, 0)),
                  _layer_spec((k, tn), layer, lambda j, i: (0, c0 + j))],
        out_specs=pl.BlockSpec((tm, tn), lambda j, i: (i, j)),
        out_shape=jax.ShapeDtypeStruct((m, n), out_dtype), name=name,
        compiler_params=_params(("parallel", "parallel"), blk),
    )(a, w_in)


MXU_COLS = 256


def _skewed(i, nt, bufs, compute_chunks, epilogue_chunks):
    def run(cur, prv, do_compute, do_epilogue):
        cs = compute_chunks(cur, prv) if do_compute else []
        es = epilogue_chunks(prv) if do_epilogue else []
        per = -(-len(es) // len(cs)) if cs else len(es)
        for k in range(max(len(cs), 1)):
            if cs:
                cs[k]()
            for e in es[k * per:(k + 1) * per]:
                e()

    parity = lax.rem(i, 2)
    middle = (i > 0) & (i < nt)
    pl.when(i == 0)(lambda: run(bufs[0], None, True, False))
    pl.when(middle & (parity == 0))(lambda: run(bufs[0], bufs[1], True, True))
    pl.when(middle & (parity == 1))(lambda: run(bufs[1], bufs[0], True, True))
    pl.when(i == nt)(lambda: run(None, bufs[1 - nt % 2], False, True))


def _skewed_in(tm_width, nt):
    return pl.BlockSpec(tm_width, lambda j, i: (jnp.minimum(i, nt - 1), 0))


def _dil_proj_kernel(a_ref, w_ref, cos_ref, lo_ref, hi_ref, o_ref, acc_a, acc_b, *, nt):
    tm = o_ref.shape[0]
    is_v = lax.rem(pl.program_id(0), 3) == 2

    def compute_chunks(cur, prv):
        def chunk(c):
            cs = slice(c * MXU_COLS, (c + 1) * MXU_COLS)
            cur[:, cs] = _dot(a_ref[...], w_ref[:, cs])
        return [functools.partial(chunk, c) for c in range(DIL_WIDTH // MXU_COLS)]

    def epilogue_chunks(prv):
        half = tm // 2

        def chunk(h, r):
            rows = slice(r * half, (r + 1) * half)
            sl = slice(h * DIL_DIM, (h + 1) * DIL_DIM)
            cos = jnp.where(is_v, 1.0, cos_ref[rows, :])
            lo = jnp.where(is_v, 0.0, lo_ref[rows, :])
            hi = jnp.where(is_v, 0.0, hi_ref[rows, :])
            o_ref[rows, sl] = _rope_lanes(prv[rows, sl], cos, lo, hi, DIL_ROT // 2).astype(o_ref.dtype)
        return [functools.partial(chunk, h, r) for h in range(DIL_HEADS) for r in range(2)]

    _skewed(pl.program_id(1), nt, (acc_a, acc_b), compute_chunks, epilogue_chunks)


def _dil_proj(x_bf, w_in, layer, tabs, tm):
    m, k = x_bf.shape
    n = len(DIL_CONFIGS) * 3 * DIL_WIDTH
    tn = DIL_WIDTH
    nt = m // tm
    c0 = OFF_DIL // tn
    prev_row = lambda j, i: (jnp.maximum(i - 1, 0), 0)
    tab_spec = pl.BlockSpec((tm, LANES), prev_row)
    blk = (_nbytes((tm, k), BF16) + _nbytes((k, tn), BF16) + _nbytes((tm, tn), BF16)
           + 3 * _nbytes((tm, LANES), F32) + _nbytes((tm, tn), F32))
    return pl.pallas_call(
        functools.partial(_dil_proj_kernel, nt=nt), grid=(n // tn, nt + 1),
        in_specs=[_skewed_in((tm, k), nt),
                  _layer_spec((k, tn), layer, lambda j, i: (0, c0 + j)),
                  tab_spec, tab_spec, tab_spec],
        out_specs=pl.BlockSpec((tm, tn), lambda j, i: (jnp.maximum(i - 1, 0), j)),
        out_shape=jax.ShapeDtypeStruct((m, n), BF16),
        scratch_shapes=[pltpu.VMEM((tm, tn), F32)] * 2, name="dil_proj",
        compiler_params=_params(("parallel", "arbitrary"), blk),
    )(x_bf, w_in, *tabs)


def _latent_kernel(x_ref, wlat_ref, gq_ref, gkv_ref, wuq_ref, wukv_ref,
                   cos_ref, lo_ref, hi_ref, q_ref, k_ref, v_ref):
    lat = _dot(x_ref[...], wlat_ref[...])
    cos, lo, hi = cos_ref[...], lo_ref[...], hi_ref[...]

    def rms(c, g):
        return c * lax.rsqrt(jnp.mean(jnp.square(c), axis=-1, keepdims=True) + RMS_EPS) * g

    def rope(v):
        return _rope_lanes(v, cos, lo, hi, MLA_ROPE // 2)

    cq = rms(lat[:, :MLA_LORA], gq_ref[...]).astype(BF16)
    ckv = rms(lat[:, MLA_LORA:2 * MLA_LORA], gkv_ref[...]).astype(BF16)
    kpe = rope(lat[:, 2 * MLA_LORA:]).astype(BF16)
    q = _dot(cq, wuq_ref[...])
    kv = _dot(ckv, wukv_ref[...])
    for h in range(MLA_HEADS):
        b0 = h * 2 * LANES
        q_ref[:, b0:b0 + LANES] = q[:, b0:b0 + LANES].astype(BF16)
        q_ref[:, b0 + LANES:b0 + 2 * LANES] = rope(q[:, b0 + LANES:b0 + 2 * LANES]).astype(BF16)
        k_ref[:, b0:b0 + LANES] = kv[:, h * LANES:(h + 1) * LANES].astype(BF16)
        k_ref[:, b0 + LANES:b0 + 2 * LANES] = kpe
    v_ref[...] = kv[:, MLA_HEADS * MLA_NOPE:].astype(BF16)


def _latent(x_bf, w_lat, gq, gkv, w_uq, w_ukv, layer, tabs, tm):
    m, k = x_bf.shape
    nl = w_lat.shape[2]
    hq = MLA_HEADS * 2 * LANES
    hv = MLA_HEADS * MLA_V
    zero2 = lambda i: (0, 0)
    row = lambda w: pl.BlockSpec((tm, w), lambda i: (i, 0))
    blk = (_nbytes((tm, k), BF16) + _nbytes((k, nl), BF16) + 2 * _nbytes((MLA_LORA, hq), BF16)
           + 2 * _nbytes((tm, hq), BF16) + _nbytes((tm, hv), BF16) + 3 * _nbytes((tm, LANES), F32))
    return pl.pallas_call(
        _latent_kernel, grid=(m // tm,),
        in_specs=[row(k), _layer_spec((k, nl), layer, zero2),
                  _layer_spec((1, MLA_LORA), layer, zero2), _layer_spec((1, MLA_LORA), layer, zero2),
                  _layer_spec((MLA_LORA, hq), layer, zero2), _layer_spec((MLA_LORA, hq), layer, zero2),
                  row(LANES), row(LANES), row(LANES)],
        out_specs=[row(hq), row(hq), row(hv)],
        out_shape=[jax.ShapeDtypeStruct((m, hq), BF16), jax.ShapeDtypeStruct((m, hq), BF16),
                   jax.ShapeDtypeStruct((m, hv), BF16)],
        name="mla_latent",
        compiler_params=_params(("parallel",), blk),
    )(x_bf, w_lat, gq, gkv, w_uq, w_ukv, *tabs)


def _mla_attn_kernel(q_ref, k_ref, v_ref, o_ref, *, tq, scale):
    s = q_ref.shape[0]
    for qi in range(s // tq):
        s0, s1 = qi * tq, (qi + 1) * tq
        sc = _dot_nt(q_ref[s0:s1, :], k_ref[:s1, :]) * scale
        row = lax.broadcasted_iota(jnp.int32, (tq, s1), 0) + s0
        col = lax.broadcasted_iota(jnp.int32, (tq, s1), 1)
        sc = jnp.where(col <= row, sc, NEG_INF)
        mx = jnp.max(sc, axis=-1, keepdims=True)
        p = jnp.exp(sc - mx)
        den = jnp.sum(p, axis=-1, keepdims=True)
        o = _dot(p.astype(BF16), v_ref[:s1, :])
        o_ref[s0:s1, :] = (o / den).astype(o_ref.dtype)


def _mla_attention(q, k, v, batch, seq, tq=256):
    m = q.shape[0]
    dq = 2 * LANES
    blk = 2 * _nbytes((seq, dq), BF16) + 2 * _nbytes((seq, MLA_V), BF16)
    return pl.pallas_call(
        functools.partial(_mla_attn_kernel, tq=tq, scale=MLA_QK ** -0.5),
        grid=(batch, MLA_HEADS),
        in_specs=[pl.BlockSpec((seq, dq), lambda b, h: (b, h)),
                  pl.BlockSpec((seq, dq), lambda b, h: (b, h)),
                  pl.BlockSpec((seq, MLA_V), lambda b, h: (b, h))],
        out_specs=pl.BlockSpec((seq, MLA_V), lambda b, h: (b, h)),
        out_shape=jax.ShapeDtypeStruct((m, MLA_HEADS * MLA_V), BF16), name="mla_attention",
        compiler_params=_params(("parallel", "parallel"), blk + 4 * _nbytes((tq, seq), F32)),
    )(q, k, v)


DIL_UNROLL = 8


def _dil_attn_kernel(q1, k1, v1, q2, k2, v2, q3, k3, v3, o_ref, stage, o_acc, lse_acc, *, seq, scale):
    nb = DIL_BLOCK
    diff = (lax.broadcasted_iota(jnp.int32, (nb, 2 * nb), 0)
            - lax.broadcasted_iota(jnp.int32, (nb, 2 * nb), 1))
    band = (diff + nb >= 0) & (diff <= 0)
    causal = (lax.broadcasted_iota(jnp.int32, (nb, nb), 1)
              <= lax.broadcasted_iota(jnp.int32, (nb, nb), 0))

    def attend_many(gi, blocks):
        scs = [jnp.where(mask, _dot_nt(q, k) * scale, NEG_INF) for _, q, k, _, mask in blocks]
        mxs = [jnp.max(sc, axis=-1, keepdims=True) for sc in scs]
        es = [jnp.exp(sc - mx) for sc, mx in zip(scs, mxs)]
        dens = [jnp.sum(e, axis=-1, keepdims=True) for e in es]
        ps = [(e / den).astype(BF16) for e, den in zip(es, dens)]
        outs = [_dot(p, blk[3]) for p, blk in zip(ps, blocks)]
        for blk, o, mx, den in zip(blocks, outs, mxs, dens):
            o_acc[gi, blk[0], :] = o
            lse_acc[gi, blk[0], :] = jnp.broadcast_to(mx + jnp.log(den), (nb, LANES))

    def block_body(it, carry):
        blocks = []
        for u in range(DIL_UNROLL):
            blk = it * DIL_UNROLL + u
            q0 = pl.multiple_of(blk * nb, nb)
            if u == 0:
                k0 = pl.multiple_of(jnp.maximum(blk - 1, 0) * nb, nb)
                rel = diff + (q0 - k0)
                mask = (rel >= 0) & (rel <= nb)
            else:
                k0 = pl.multiple_of(blk * nb - nb, nb)
                mask = band
            blocks.append((pl.ds(q0, nb), q1[pl.ds(q0, nb), :], k1[pl.ds(k0, 2 * nb), :],
                           v1[pl.ds(k0, 2 * nb), :], mask))
        attend_many(0, blocks)
        return carry

    lax.fori_loop(0, seq // (nb * DIL_UNROLL), block_body, 0)

    for gi, (q_ref, k_ref, v_ref) in ((1, (q2, k2, v2)), (2, (q3, k3, v3))):
        dil = DIL_CONFIGS[gi][1]
        length = seq // dil
        nblk = length // nb
        per_iter = DIL_UNROLL // nblk
        stage[0] = q_ref[...].astype(F32)
        stage[1] = k_ref[...].astype(F32)
        stage[2] = v_ref[...].astype(F32)

        for it in range(dil // per_iter):
            blocks = []
            for r in range(it * per_iter, (it + 1) * per_iter):
                qc, kc, vc = [stage[j, pl.ds(r, length, stride=dil), :].astype(BF16) for j in range(3)]
                for blk in range(nblk):
                    rows = pl.ds(r + blk * nb * dil, nb, stride=dil)
                    q_blk = qc[blk * nb:(blk + 1) * nb]
                    if blk == 0:
                        blocks.append((rows, q_blk, kc[0:nb], vc[0:nb], causal))
                    else:
                        blocks.append((rows, q_blk, kc[(blk - 1) * nb:(blk + 1) * nb],
                                       vc[(blk - 1) * nb:(blk + 1) * nb], band))
            attend_many(gi, blocks)

    chunk = 2 * nb

    def merge_body(c, carry):
        rows = pl.ds(pl.multiple_of(c * chunk, chunk), chunk)
        la, lb, lc = lse_acc[0, rows, :], lse_acc[1, rows, :], lse_acc[2, rows, :]
        mx = jnp.maximum(jnp.maximum(la, lb), lc)
        ea, eb, ec = jnp.exp(la - mx), jnp.exp(lb - mx), jnp.exp(lc - mx)
        den = ea + eb + ec
        out = ((ea / den) * o_acc[0, rows, :] + (eb / den) * o_acc[1, rows, :]
               + (ec / den) * o_acc[2, rows, :])
        o_ref[rows, :] = out.astype(o_ref.dtype)
        return carry

    lax.fori_loop(0, seq // chunk, merge_body, 0)


def _dil_attention(qkv, batch, seq):
    m = qkv.shape[0]
    col_blocks = DIL_WIDTH // LANES
    specs = [pl.BlockSpec((seq, LANES), lambda b, h, c=(g * 3 + j) * col_blocks: (b, c + h))
             for g in range(len(DIL_CONFIGS)) for j in range(3)]
    blk = 10 * _nbytes((seq, LANES), BF16)
    scratch = 9 * _nbytes((seq, LANES), F32)
    return pl.pallas_call(
        functools.partial(_dil_attn_kernel, seq=seq, scale=DIL_DIM ** -0.5),
        grid=(batch, DIL_HEADS),
        in_specs=specs,
        out_specs=pl.BlockSpec((seq, LANES), lambda b, h: (b, h)),
        out_shape=jax.ShapeDtypeStruct((m, DIL_WIDTH), BF16),
        scratch_shapes=[pltpu.VMEM((3, seq, LANES), F32)] * 3,
        name="dil_attention",
        compiler_params=_params(("parallel", "parallel"), blk + scratch // 2),
    )(*([qkv] * 9))


POOL_HALO = 16


def _pool_kernel(cur_ref, halo_ref, w_ref, scale_ref, o_ref, buf, *, ts):
    i = pl.program_id(1)
    buf[0:POOL_HALO, :] = jnp.where(i > 0, halo_ref[...], 0.0)
    buf[POOL_HALO:, :] = cur_ref[...]
    t = i * ts + lax.broadcasted_iota(jnp.int32, (ts, 1), 0)
    for g, win in enumerate(POOL_WINDOWS):
        sl = slice(g * POOL_GROUP, (g + 1) * POOL_GROUP)
        tok = buf[POOL_HALO:POOL_HALO + ts, sl]
        acc = tok
        for j in range(1, win):
            acc = acc + buf[POOL_HALO - j:POOL_HALO - j + ts, sl]
        cnt = jnp.minimum(t + 1, win).astype(F32)
        d = (acc / cnt - tok).astype(BF16)
        o_ref[:, sl] = (_dot(d, w_ref[g]) * scale_ref[:, sl]).astype(o_ref.dtype)


def _pool_mixer(u_pool, w_grp, scale, layer, batch, seq, ts=256):
    m, c = u_pool.shape
    nt = seq // ts
    hb = ts // POOL_HALO
    blk = (_nbytes((ts, c), F32) + _nbytes((POOL_HALO, c), F32) + _nbytes(w_grp.shape[1:], BF16)
           + _nbytes((ts, c), BF16) + _nbytes((ts + POOL_HALO, c), F32))
    return pl.pallas_call(
        functools.partial(_pool_kernel, ts=ts), grid=(batch, nt),
        in_specs=[pl.BlockSpec((ts, c), lambda b, i: (b * nt + i, 0)),
                  pl.BlockSpec((POOL_HALO, c), lambda b, i: ((b * nt + i) * hb - jnp.minimum(i, 1), 0)),
                  _layer_spec(w_grp.shape[1:], layer, lambda b, i: (0, 0, 0)),
                  _layer_spec((1, c), layer, lambda b, i: (0, 0))],
        out_specs=pl.BlockSpec((ts, c), lambda b, i: (b * nt + i, 0)),
        out_shape=jax.ShapeDtypeStruct((m, c), BF16),
        scratch_shapes=[pltpu.VMEM((ts + POOL_HALO, c), F32)], name="pool_mixer",
        compiler_params=_params(("parallel", "parallel"), blk),
    )(u_pool, u_pool, w_grp, scale)


CONV_HALO = 32


def _conv_kernel(cur_ref, halo_ref, w_ref, b_ref, g_ref, beta_ref, o_ref, buf, hbuf, *, ts):
    i = pl.program_id(1)
    c = CONV_WIDTH

    def glu(u):
        return u[:, :c] * jax.nn.sigmoid(u[:, c:])

    buf[0:CONV_HALO, :] = jnp.where(i > 0, glu(halo_ref[...]), 0.0)
    buf[CONV_HALO:, :] = glu(cur_ref[...])
    off = CONV_HALO - (CONV_K - 1)
    for cc in range(c // LANES):
        sl = slice(cc * LANES, (cc + 1) * LANES)
        acc = jnp.zeros((ts, LANES), F32)
        for k in range(CONV_K):
            acc = acc + w_ref[k:k + 1, sl] * buf[off + k:off + k + ts, sl]
        hbuf[:, sl] = acc + b_ref[:, sl]
    h = hbuf[...]
    mu = jnp.mean(h, axis=-1, keepdims=True)
    var = jnp.mean(jnp.square(h - mu), axis=-1, keepdims=True)
    y = (h - mu) * lax.rsqrt(var + LN_EPS) * g_ref[...] + beta_ref[...]
    o_ref[...] = (y * jax.nn.sigmoid(y)).astype(o_ref.dtype)


def _conv_mixer(u_conv, w_dw, b_dw, ln_g, ln_b, layer, batch, seq, ts=256):
    m, c2 = u_conv.shape
    c = c2 // 2
    nt = seq // ts
    hb = ts // CONV_HALO
    vec = _layer_spec((1, c), layer, lambda b, i: (0, 0))
    blk = (_nbytes((ts, c2), F32) + _nbytes((CONV_HALO, c2), F32) + _nbytes(w_dw.shape[1:], F32)
           + _nbytes((ts, c), BF16) + 2 * _nbytes((ts + CONV_HALO, c), F32))
    return pl.pallas_call(
        functools.partial(_conv_kernel, ts=ts), grid=(batch, nt),
        in_specs=[pl.BlockSpec((ts, c2), lambda b, i: (b * nt + i, 0)),
                  pl.BlockSpec((CONV_HALO, c2), lambda b, i: ((b * nt + i) * hb - jnp.minimum(i, 1), 0)),
                  _layer_spec(w_dw.shape[1:], layer, lambda b, i: (0, 0)), vec, vec, vec],
        out_specs=pl.BlockSpec((ts, c), lambda b, i: (b * nt + i, 0)),
        out_shape=jax.ShapeDtypeStruct((m, c), BF16),
        scratch_shapes=[pltpu.VMEM((ts + CONV_HALO, c), F32), pltpu.VMEM((ts, c), F32)],
        name="conv_mixer",
        compiler_params=_params(("parallel", "parallel"), blk),
    )(u_conv, u_conv, w_dw, b_dw, ln_g, ln_b)


def _merge_kernel(x_ref, ha, hb, hc, hd, g0, g1, g2, g3, pa, pb, pc, pd, bias_ref, o_ref):
    x = x_ref[...]
    acc = None
    for b, (h_ref, wg_ref, wp_ref) in enumerate(((ha, g0, pa), (hb, g1, pb), (hc, g2, pc), (hd, g3, pd))):
        gate = jax.nn.sigmoid(_dot(x, wg_ref[...]) + bias_ref[b:b + 1, :])
        term = gate * _dot(h_ref[...], wp_ref[...])
        acc = term if acc is None else acc + term
    o_ref[...] = acc.astype(o_ref.dtype)


def _merge(x_bf, hs, w_in, w_projs, b_gate, layer, tm=512, tn=512):
    m, d = x_bf.shape
    kb = hs[0].shape[1]
    nj = d // tn
    c0 = OFF_GATE // tn
    gate_spec = lambda b: _layer_spec((d, tn), layer, lambda j, i: (0, c0 + b * nj + j))
    blk = (_nbytes((tm, d), BF16) + 4 * _nbytes((tm, kb), BF16) + 4 * _nbytes((d, tn), BF16)
           + 4 * _nbytes((kb, tn), BF16) + _nbytes((tm, tn), BF16) + 4 * _nbytes((tm, tn), F32))
    return pl.pallas_call(
        _merge_kernel, grid=(nj, m // tm),
        in_specs=([pl.BlockSpec((tm, d), lambda j, i: (i, 0))]
                  + [pl.BlockSpec((tm, kb), lambda j, i: (i, 0))] * 4
                  + [gate_spec(b) for b in range(N_BRANCHES)]
                  + [_layer_spec((kb, tn), layer, lambda j, i: (0, j))] * 4
                  + [_layer_spec((N_BRANCHES, tn), layer, lambda j, i: (0, j))]),
        out_specs=pl.BlockSpec((tm, tn), lambda j, i: (i, j)),
        out_shape=jax.ShapeDtypeStruct((m, d), BF16), name="gated_merge",
        compiler_params=_params(("parallel", "parallel"), blk),
    )(x_bf, *hs, w_in, w_in, w_in, w_in, *w_projs, b_gate)


def _proj_ln_kernel(a_ref, w_ref, res_ref, g_ref, b_ref, of_ref, ob_ref, acc_ref, *, alpha):
    k = pl.program_id(1)

    @pl.when(k == 0)
    def _():
        acc_ref[...] = jnp.zeros_like(acc_ref)

    acc_ref[...] += _dot(a_ref[...], w_ref[...])

    @pl.when(k == pl.num_programs(1) - 1)
    def _():
        y = alpha * res_ref[...] + acc_ref[...]
        mu = jnp.mean(y, axis=-1, keepdims=True)
        var = jnp.mean(jnp.square(y - mu), axis=-1, keepdims=True)
        out = (y - mu) * lax.rsqrt(var + LN_EPS) * g_ref[...] + b_ref[...]
        of_ref[...] = out
        ob_ref[...] = out.astype(BF16)


def _proj_ln(a, w, res, g, b, layer, alpha, tm, tk, name):
    m, k = a.shape
    d = w.shape[2]
    vec = _layer_spec((1, d), layer, lambda i, kk: (0, 0))
    row = pl.BlockSpec((tm, d), lambda i, kk: (i, 0))
    blk = (_nbytes((tm, tk), BF16) + _nbytes((tk, d), BF16) + 2 * _nbytes((tm, d), F32)
           + _nbytes((tm, d), BF16) + _nbytes((tm, d), F32))
    return pl.pallas_call(
        functools.partial(_proj_ln_kernel, alpha=alpha), grid=(m // tm, k // tk),
        in_specs=[pl.BlockSpec((tm, tk), lambda i, kk: (i, kk)),
                  _layer_spec((tk, d), layer, lambda i, kk: (kk, 0)), row, vec, vec],
        out_specs=[row, row],
        out_shape=[jax.ShapeDtypeStruct((m, d), F32), jax.ShapeDtypeStruct((m, d), BF16)],
        scratch_shapes=[pltpu.VMEM((tm, d), F32)], name=name,
        compiler_params=_params(("parallel", "arbitrary"), blk),
    )(a, w, res, g, b)


FFN_HALO = 8


def _ffn_up_kernel(x_ref, wv_ref, wg_ref, dwv_ref, dwg_ref, bv_ref, bg_ref, o_ref,
                   wv_bf, wg_bf, bufv, bufg, *, tm, tiles_per_seq):
    i = pl.program_id(1)
    first = lax.rem(i, tiles_per_seq) == 0

    @pl.when(i == 0)
    def _():
        wv_bf[...] = wv_ref[...].astype(BF16)
        wg_bf[...] = wg_ref[...].astype(BF16)

    @pl.when(first)
    def _():
        bufv[0:FFN_HALO, :] = jnp.zeros((FFN_HALO, bufv.shape[1]), F32)
        bufg[0:FFN_HALO, :] = jnp.zeros((FFN_HALO, bufg.shape[1]), F32)

    @pl.when(jnp.logical_not(first))
    def _():
        bufv[0:FFN_HALO, :] = bufv[tm:tm + FFN_HALO, :]
        bufg[0:FFN_HALO, :] = bufg[tm:tm + FFN_HALO, :]

    x = x_ref[...]
    bufv[FFN_HALO:, :] = _dot(x, wv_bf[...])
    bufg[FFN_HALO:, :] = _dot(x, wg_bf[...])

    def conv(buf, w_ref, b_ref):
        off = FFN_HALO - (FFN_K - 1)
        acc = w_ref[0:1, :] * buf[off:off + tm, :]
        for k in range(1, FFN_K):
            acc = acc + w_ref[k:k + 1, :] * buf[off + k:off + k + tm, :]
        return acc + b_ref[...]

    val = conv(bufv, dwv_ref, bv_ref)
    gate = conv(bufg, dwg_ref, bg_ref)
    o_ref[...] = (val * (gate * jax.nn.sigmoid(gate))).astype(o_ref.dtype)


def _ffn_up(x_bf, w_up, dw, dw_b, layer, seq, tm=512, tn=512):
    m, d = x_bf.shape
    f = w_up.shape[2] // 2
    nj = f // tn
    blk = (_nbytes((tm, d), BF16) + 2 * _nbytes((d, tn), F32) + _nbytes((d, tn), BF16)
           + _nbytes((tm, tn), BF16) + _nbytes((tm + FFN_HALO, tn), F32))
    return pl.pallas_call(
        functools.partial(_ffn_up_kernel, tm=tm, tiles_per_seq=seq // tm), grid=(nj, m // tm),
        in_specs=[pl.BlockSpec((tm, d), lambda j, i: (i, 0)),
                  _layer_spec((d, tn), layer, lambda j, i: (0, j)),
                  _layer_spec((d, tn), layer, lambda j, i: (0, nj + j)),
                  _layer_spec((FFN_K, tn), layer, lambda j, i: (0, j)),
                  _layer_spec((FFN_K, tn), layer, lambda j, i: (0, nj + j)),
                  _layer_spec((1, tn), layer, lambda j, i: (0, j)),
                  _layer_spec((1, tn), layer, lambda j, i: (0, nj + j))],
        out_specs=pl.BlockSpec((tm, tn), lambda j, i: (i, j)),
        out_shape=jax.ShapeDtypeStruct((m, f), BF16),
        scratch_shapes=[pltpu.VMEM((d, tn), BF16)] * 2 + [pltpu.VMEM((tm + FFN_HALO, tn), F32)] * 2,
        name="ffn_up",
        compiler_params=_params(("parallel", "arbitrary"), blk),
    )(x_bf, w_up, w_up, dw, dw, dw_b, dw_b)


def _prep_in_proj(w_in):
    pe1 = 2 * MLA_LORA + MLA_ROPE
    w_lat = jnp.pad(w_in[:, :, :pe1], ((0, 0), (0, 0), (0, LANES - MLA_ROPE))).astype(BF16)
    return w_lat, w_in[:, :, pe1:].astype(BF16)


def _prep_mla_up(mla_w_uq, mla_w_ukv):
    depth = mla_w_uq.shape[0]
    w_uq = jnp.pad(mla_w_uq.reshape(depth, MLA_LORA, MLA_HEADS, MLA_QK),
                   ((0, 0), (0, 0), (0, 0), (0, 2 * LANES - MLA_QK)))
    w_uq = w_uq.reshape(depth, MLA_LORA, -1).astype(BF16)
    ukv = mla_w_ukv.reshape(depth, MLA_LORA, MLA_HEADS, MLA_NOPE + MLA_V)
    w_ukv = jnp.concatenate([ukv[..., :MLA_NOPE].reshape(depth, MLA_LORA, -1),
                             ukv[..., MLA_NOPE:].reshape(depth, MLA_LORA, -1)], axis=2).astype(BF16)
    return w_uq, w_ukv


def kernel(x, positions, w_in, b_gate, mla_gq, mla_gkv, mla_w_uq, mla_w_ukv, mla_w_proj, pool_w, pool_scale, pool_w_proj, conv_dw, conv_dw_b, conv_ln_g, conv_ln_b, conv_w_proj, dil_w_proj, mix_w_out, ln1_g, ln1_b, ffn_w_up, ffn_dw, ffn_dw_b, ffn_w_down, ln2_g, ln2_b):
    batch, seq, d_model = x.shape
    depth = w_in.shape[0]
    m = batch * seq
    alpha = (2 * depth) ** 0.25
    ffn_dim = ffn_w_down.shape[1]

    tabs = _rope_tables(positions)
    mla_tabs, dil_tabs = tabs[:3], tabs[3:]

    w_lat, w_in_r = _prep_in_proj(w_in)
    w_uq, w_ukv = _prep_mla_up(mla_w_uq, mla_w_ukv)
    w_projs = tuple(w.astype(BF16) for w in (mla_w_proj, pool_w_proj, conv_w_proj, dil_w_proj))
    w_out = mix_w_out.astype(BF16)
    w_down = ffn_w_down.astype(BF16)
    pool_w_bf = pool_w.astype(BF16)
    conv_dw_p = jnp.pad(conv_dw, ((0, 0), (0, 1), (0, 0)))
    vec = lambda v: v[:, None, :]
    gq, gkv, p_scale = vec(mla_gq), vec(mla_gkv), vec(pool_scale)
    c_b, c_g, c_beta = vec(conv_dw_b), vec(conv_ln_g), vec(conv_ln_b)
    g1, b1, g2, b2 = vec(ln1_g), vec(ln1_b), vec(ln2_g), vec(ln2_b)
    f_b = vec(ffn_dw_b)

    xf = x.reshape(m, d_model)
    xb = xf.astype(BF16)
    for l in range(depth):
        q, k, v = _latent(xb, w_lat, gq, gkv, w_uq, w_ukv, l, mla_tabs, tm=512)
        h_a = _mla_attention(q, k, v, batch, seq)

        u_pool = _in_proj(xb, w_in_r, l, OFF_POOL, POOL_WIDTH, F32, 1024, 1024, "pool_in")
        h_b = _pool_mixer(u_pool, pool_w_bf, p_scale, l, batch, seq)

        u_conv = _in_proj(xb, w_in_r, l, OFF_CONV, 2 * CONV_WIDTH, F32, 1024, 1024, "conv_in")
        h_c = _conv_mixer(u_conv, conv_dw_p, c_b, c_g, c_beta, l, batch, seq)

        qkv = _dil_proj(xb, w_in_r, l, dil_tabs, tm=1024)
        h_d = _dil_attention(qkv, batch, seq)

        merged = _merge(xb, (h_a, h_b, h_c, h_d), w_in_r, w_projs, b_gate, l)
        xf, xb = _proj_ln(merged, w_out, xf, g1, b1, l, alpha, tm=512, tk=d_model, name="mix_out_ln1")

        hid = _ffn_up(xb, ffn_w_up, ffn_dw, f_b, l, seq)
        xf, xb = _proj_ln(hid, w_down, xf, g2, b2, l, alpha, tm=512, tk=ffn_dim // 4, name="ffn_down_ln2")
    return xf.reshape(batch, seq, d_model)
```

```python
import functools

import jax
import jax.numpy as jnp
from jax import lax
from jax.experimental import pallas as pl
from jax.experimental.pallas import tpu as pltpu

MLA_HEADS = 8
MLA_LORA = 512
MLA_NOPE = 128
MLA_ROPE = 64
MLA_V = 128
MLA_QK = MLA_NOPE + MLA_ROPE
POOL_WINDOWS = (2, 4, 8, 16)
POOL_GROUP = 256
POOL_WIDTH = 1024
CONV_WIDTH = 1024
CONV_K = 31
DIL_CONFIGS = ((128, 1), (512, 4), (2048, 16))
DIL_HEADS = 8
DIL_DIM = 128
DIL_WIDTH = DIL_HEADS * DIL_DIM
DIL_ROT = DIL_DIM // 4
DIL_BLOCK = 128
N_BRANCHES = 4
ROPE_THETA = 500000.0
FFN_K = 3
LN_EPS = 1e-5
RMS_EPS = 1e-6
NEG_INF = -1e30
LOG2_E = 1.4426950408889634

OFF_POOL = 0
OFF_CONV = OFF_POOL + POOL_WIDTH
OFF_DIL = OFF_CONV + 2 * CONV_WIDTH
OFF_GATE = OFF_DIL + len(DIL_CONFIGS) * 3 * DIL_WIDTH

LANES = 128
V7X_VMEM_BYTES = 64 * 1024 * 1024
VMEM_CAP = V7X_VMEM_BYTES - 8 * 1024 * 1024

BF16 = jnp.bfloat16
F32 = jnp.float32


def _params(semantics, block_bytes, flags=None):
    limit = min(VMEM_CAP, max(32 * 1024 * 1024, 2 * block_bytes + 16 * 1024 * 1024))
    return pltpu.CompilerParams(dimension_semantics=semantics, vmem_limit_bytes=limit, flags=flags)


def _nbytes(shape, dtype):
    n = 1
    for s in shape:
        n *= s
    return n * jnp.dtype(dtype).itemsize


def _dot(a, b):
    return jnp.dot(a, b, preferred_element_type=F32)


def _dot_nt(a, b):
    return lax.dot_general(a, b, (((1,), (1,)), ((), ())), preferred_element_type=F32)


def _layer_spec(block, layer, index_map):
    return pl.BlockSpec((None,) + tuple(block), lambda *g: (layer,) + tuple(index_map(*g)))


def _rope_lanes(x, cos, sin_lo, sin_hi, half):
    return (x * cos + pltpu.roll(x, LANES - half, 1) * sin_lo
            + pltpu.roll(x, half, 1) * sin_hi)


def _rope_table_kernel(pos_ref, c_ref, cos_a, lo_a, hi_a, cos_b, lo_b, hi_b):
    pos = pos_ref[...]
    for base, (co, lo, hi) in ((0, (cos_a, lo_a, hi_a)), (3, (cos_b, lo_b, hi_b))):
        ang = pos * c_ref[base:base + 1, :]
        sn = jnp.sin(ang)
        co[...] = jnp.cos(ang)
        lo[...] = sn * c_ref[base + 1:base + 2, :]
        hi[...] = sn * c_ref[base + 2:base + 3, :]


def _lane_pattern(rot_dim):
    half = rot_dim // 2
    inv_freq = ROPE_THETA ** (-jnp.arange(half, dtype=F32) * 2.0 / rot_dim)
    lane = jnp.arange(LANES)
    freq = jnp.where(lane < rot_dim, jnp.tile(inv_freq, LANES // half), 0.0)
    lo = jnp.where(lane < half, -1.0, 0.0)
    hi = jnp.where((lane >= half) & (lane < rot_dim), 1.0, 0.0)
    return [freq.astype(F32), lo.astype(F32), hi.astype(F32)]


def _rope_tables(positions):
    m = positions.size
    pos = jnp.broadcast_to(positions.reshape(m, 1).astype(F32), (m, LANES))
    consts = jnp.stack(_lane_pattern(MLA_ROPE) + _lane_pattern(DIL_ROT)
                       + [jnp.zeros((LANES,), F32)] * 2)
    tm = 1024
    spec = pl.BlockSpec((tm, LANES), lambda i: (i, 0))
    out = jax.ShapeDtypeStruct((m, LANES), F32)
    return pl.pallas_call(
        _rope_table_kernel, grid=(m // tm,),
        in_specs=[spec, pl.BlockSpec((8, LANES), lambda i: (0, 0))],
        out_specs=[spec] * 6, out_shape=[out] * 6, name="rope_tables",
        compiler_params=_params(("parallel",), 7 * _nbytes((tm, LANES), F32)),
    )(pos, consts)


def _mm_kernel(a_ref, w_ref, o_ref):
    o_ref[...] = _dot(a_ref[...], w_ref[...]).astype(o_ref.dtype)


def _in_proj(a, w_in, layer, col_off, n, out_dtype, tm, tn, name):
    m, k = a.shape
    c0 = col_off // tn
    blk = (_nbytes((tm, k), a.dtype) + _nbytes((k, tn), w_in.dtype) + _nbytes((tm, tn), out_dtype))
    return pl.pallas_call(
        _mm_kernel, grid=(n // tn, m // tm),
        in_specs=[pl.BlockSpec((tm, k), lambda j, i: (i, 0)),
                  _layer_spec((k, tn), layer, lambda j, i: (0, c0 + j))],
        out_specs=pl.BlockSpec((tm, tn), lambda j, i: (i, j)),
        out_shape=jax.ShapeDtypeStruct((m, n), out_dtype), name=name,
        compiler_params=_params(("parallel", "parallel"), blk),
    )(a, w_in)


MXU_COLS = 256


def _skewed(i, nt, bufs, compute_chunks, epilogue_chunks):
    def run(cur, prv, do_compute, do_epilogue):
        cs = compute_chunks(cur, prv) if do_compute else []
        es = epilogue_chunks(prv) if do_epilogue else []
        per = -(-len(es) // len(cs)) if cs else len(es)
        for k in range(max(len(cs), 1)):
            if cs:
                cs[k]()
            for e in es[k * per:(k + 1) * per]:
                e()

    parity = lax.rem(i, 2)
    middle = (i > 0) & (i < nt)
    pl.when(i == 0)(lambda: run(bufs[0], None, True, False))
    pl.when(middle & (parity == 0))(lambda: run(bufs[0], bufs[1], True, True))
    pl.when(middle & (parity == 1))(lambda: run(bufs[1], bufs[0], True, True))
    pl.when(i == nt)(lambda: run(None, bufs[1 - nt % 2], False, True))


def _skewed_in(tm_width, nt):
    return pl.BlockSpec(tm_width, lambda j, i: (jnp.minimum(i, nt - 1), 0))


def _dil_proj_kernel(a_ref, w_ref, cos_ref, lo_ref, hi_ref, o_ref, acc_a, acc_b, *, nt):
    tm = o_ref.shape[0]
    is_v = lax.rem(pl.program_id(0), 3) == 2

    def compute_chunks(cur, prv):
        def chunk(c):
            cs = slice(c * MXU_COLS, (c + 1) * MXU_COLS)
            cur[:, cs] = _dot(a_ref[...], w_ref[:, cs])
        return [functools.partial(chunk, c) for c in range(DIL_WIDTH // MXU_COLS)]

    def epilogue_chunks(prv):
        half = tm // 2

        def chunk(h, r):
            rows = slice(r * half, (r + 1) * half)
            sl = slice(h * DIL_DIM, (h + 1) * DIL_DIM)
            cos = jnp.where(is_v, 1.0, cos_ref[rows, :])
            lo = jnp.where(is_v, 0.0, lo_ref[rows, :])
            hi = jnp.where(is_v, 0.0, hi_ref[rows, :])
            o_ref[rows, sl] = _rope_lanes(prv[rows, sl], cos, lo, hi, DIL_ROT // 2).astype(o_ref.dtype)
        return [functools.partial(chunk, h, r) for h in range(DIL_HEADS) for r in range(2)]

    _skewed(pl.program_id(1), nt, (acc_a, acc_b), compute_chunks, epilogue_chunks)


def _dil_proj(x_bf, w_in, layer, tabs, tm):
    m, k = x_bf.shape
    n = len(DIL_CONFIGS) * 3 * DIL_WIDTH
    tn = DIL_WIDTH
    nt = m // tm
    c0 = OFF_DIL // tn
    prev_row = lambda j, i: (jnp.maximum(i - 1, 0), 0)
    tab_spec = pl.BlockSpec((tm, LANES), prev_row)
    blk = (_nbytes((tm, k), BF16) + _nbytes((k, tn), BF16) + _nbytes((tm, tn), BF16)
           + 3 * _nbytes((tm, LANES), F32) + _nbytes((tm, tn), F32))
    return pl.pallas_call(
        functools.partial(_dil_proj_kernel, nt=nt), grid=(n // tn, nt + 1),
        in_specs=[_skewed_in((tm, k), nt),
                  _layer_spec((k, tn), layer, lambda j, i: (0, c0 + j)),
                  tab_spec, tab_spec, tab_spec],
        out_specs=pl.BlockSpec((tm, tn), lambda j, i: (jnp.maximum(i - 1, 0), j)),
        out_shape=jax.ShapeDtypeStruct((m, n), BF16),
        scratch_shapes=[pltpu.VMEM((tm, tn), F32)] * 2, name="dil_proj",
        compiler_params=_params(("parallel", "arbitrary"), blk),
    )(x_bf, w_in, *tabs)


def _latent_kernel(x_ref, wlat_ref, gq_ref, gkv_ref, wuq_ref, wukv_ref,
                   cos_ref, lo_ref, hi_ref, q_ref, k_ref, v_ref):
    lat = _dot(x_ref[...], wlat_ref[...])
    cos, lo, hi = cos_ref[...], lo_ref[...], hi_ref[...]

    def rms(c, g):
        return c * lax.rsqrt(jnp.mean(jnp.square(c), axis=-1, keepdims=True) + RMS_EPS) * g

    def rope(v):
        return _rope_lanes(v, cos, lo, hi, MLA_ROPE // 2)

    cq = rms(lat[:, :MLA_LORA], gq_ref[...]).astype(BF16)
    ckv = rms(lat[:, MLA_LORA:2 * MLA_LORA], gkv_ref[...]).astype(BF16)
    kpe = rope(lat[:, 2 * MLA_LORA:]).astype(BF16)
    q = _dot(cq, wuq_ref[...])
    kv = _dot(ckv, wukv_ref[...])
    for h in range(MLA_HEADS):
        b0 = h * 2 * LANES
        q_ref[:, b0:b0 + LANES] = q[:, b0:b0 + LANES].astype(BF16)
        q_ref[:, b0 + LANES:b0 + 2 * LANES] = rope(q[:, b0 + LANES:b0 + 2 * LANES]).astype(BF16)
        k_ref[:, b0:b0 + LANES] = kv[:, h * LANES:(h + 1) * LANES].astype(BF16)
        k_ref[:, b0 + LANES:b0 + 2 * LANES] = kpe
    v_ref[...] = kv[:, MLA_HEADS * MLA_NOPE:].astype(BF16)


def _latent(x_bf, w_lat, gq, gkv, w_uq, w_ukv, layer, tabs, tm):
    m, k = x_bf.shape
    nl = w_lat.shape[2]
    hq = MLA_HEADS * 2 * LANES
    hv = MLA_HEADS * MLA_V
    zero2 = lambda i: (0, 0)
    row = lambda w: pl.BlockSpec((tm, w), lambda i: (i, 0))
    blk = (_nbytes((tm, k), BF16) + _nbytes((k, nl), BF16) + 2 * _nbytes((MLA_LORA, hq), BF16)
           + 2 * _nbytes((tm, hq), BF16) + _nbytes((tm, hv), BF16) + 3 * _nbytes((tm, LANES), F32))
    return pl.pallas_call(
        _latent_kernel, grid=(m // tm,),
        in_specs=[row(k), _layer_spec((k, nl), layer, zero2),
                  _layer_spec((1, MLA_LORA), layer, zero2), _layer_spec((1, MLA_LORA), layer, zero2),
                  _layer_spec((MLA_LORA, hq), layer, zero2), _layer_spec((MLA_LORA, hq), layer, zero2),
                  row(LANES), row(LANES), row(LANES)],
        out_specs=[row(hq), row(hq), row(hv)],
        out_shape=[jax.ShapeDtypeStruct((m, hq), BF16), jax.ShapeDtypeStruct((m, hq), BF16),
                   jax.ShapeDtypeStruct((m, hv), BF16)],
        name="mla_latent",
        compiler_params=_params(("parallel",), blk),
    )(x_bf, w_lat, gq, gkv, w_uq, w_ukv, *tabs)


def _mla_attn_kernel(q_ref, k_ref, v_ref, o_ref, *, tq, scale):
    s = q_ref.shape[0]
    c = scale * LOG2_E
    diag = (lax.broadcasted_iota(jnp.int32, (tq, tq), 1)
            <= lax.broadcasted_iota(jnp.int32, (tq, tq), 0))
    for qi in range(s // tq):
        s0, s1 = qi * tq, (qi + 1) * tq
        q = q_ref[s0:s1, :]
        sd = jnp.where(diag, _dot_nt(q, k_ref[s0:s1, :]), NEG_INF)
        mx = jnp.max(sd, axis=-1, keepdims=True)
        if qi > 0:
            so = _dot_nt(q, k_ref[:s0, :])
            mx = jnp.maximum(mx, jnp.max(so, axis=-1, keepdims=True))
            po = jnp.exp2((so - mx) * c)
        pd = jnp.exp2((sd - mx) * c)
        den = jnp.sum(pd, axis=-1, keepdims=True)
        o = _dot(pd.astype(BF16), v_ref[s0:s1, :])
        if qi > 0:
            den = den + jnp.sum(po, axis=-1, keepdims=True)
            o = o + _dot(po.astype(BF16), v_ref[:s0, :])
        o_ref[s0:s1, :] = (o / den).astype(o_ref.dtype)


def _mla_attention(q, k, v, batch, seq, tq=256):
    m = q.shape[0]
    dq = 2 * LANES
    blk = 2 * _nbytes((seq, dq), BF16) + 2 * _nbytes((seq, MLA_V), BF16)
    return pl.pallas_call(
        functools.partial(_mla_attn_kernel, tq=tq, scale=MLA_QK ** -0.5),
        grid=(batch, MLA_HEADS),
        in_specs=[pl.BlockSpec((seq, dq), lambda b, h: (b, h)),
                  pl.BlockSpec((seq, dq), lambda b, h: (b, h)),
                  pl.BlockSpec((seq, MLA_V), lambda b, h: (b, h))],
        out_specs=pl.BlockSpec((seq, MLA_V), lambda b, h: (b, h)),
        out_shape=jax.ShapeDtypeStruct((m, MLA_HEADS * MLA_V), BF16), name="mla_attention",
        compiler_params=_params(("parallel", "parallel"), blk + 4 * _nbytes((tq, seq), F32)),
    )(q, k, v)


DIL_UNROLL = 8


def _dil_attn_kernel(q1, k1, v1, q2, k2, v2, q3, k3, v3, o_ref, stage, o_acc, lse_acc, *, seq, scale):
    nb = DIL_BLOCK
    diff = (lax.broadcasted_iota(jnp.int32, (nb, 2 * nb), 0)
            - lax.broadcasted_iota(jnp.int32, (nb, 2 * nb), 1))
    band = (diff + nb >= 0) & (diff <= 0)
    causal = (lax.broadcasted_iota(jnp.int32, (nb, nb), 1)
              <= lax.broadcasted_iota(jnp.int32, (nb, nb), 0))

    def attend_many(gi, blocks):
        scs = [jnp.where(mask, _dot_nt(q, k), NEG_INF) for _, q, k, _, mask in blocks]
        mxs = [jnp.max(sc, axis=-1, keepdims=True) for sc in scs]
        es = [jnp.exp2((sc - mx) * (scale * LOG2_E)) for sc, mx in zip(scs, mxs)]
        dens = [jnp.sum(e, axis=-1, keepdims=True) for e in es]
        ps = [(e / den).astype(BF16) for e, den in zip(es, dens)]
        outs = [_dot(p, blk[3]) for p, blk in zip(ps, blocks)]
        for blk, o, mx, den in zip(blocks, outs, mxs, dens):
            o_acc[gi, blk[0], :] = o
            lse_acc[gi, blk[0], :] = jnp.broadcast_to(mx * scale + jnp.log(den), (nb, LANES))

    def block_body(it, carry):
        blocks = []
        for u in range(DIL_UNROLL):
            blk = it * DIL_UNROLL + u
            q0 = pl.multiple_of(blk * nb, nb)
            if u == 0:
                k0 = pl.multiple_of(jnp.maximum(blk - 1, 0) * nb, nb)
                rel = diff + (q0 - k0)
                mask = (rel >= 0) & (rel <= nb)
            else:
                k0 = pl.multiple_of(blk * nb - nb, nb)
                mask = band
            blocks.append((pl.ds(q0, nb), q1[pl.ds(q0, nb), :], k1[pl.ds(k0, 2 * nb), :],
                           v1[pl.ds(k0, 2 * nb), :], mask))
        attend_many(0, blocks)
        return carry

    lax.fori_loop(0, seq // (nb * DIL_UNROLL), block_body, 0)

    for gi, (q_ref, k_ref, v_ref) in ((1, (q2, k2, v2)), (2, (q3, k3, v3))):
        dil = DIL_CONFIGS[gi][1]
        length = seq // dil
        nblk = length // nb
        per_iter = DIL_UNROLL // nblk
        stage[0] = q_ref[...].astype(F32)
        stage[1] = k_ref[...].astype(F32)
        stage[2] = v_ref[...].astype(F32)

        for it in range(dil // per_iter):
            blocks = []
            for r in range(it * per_iter, (it + 1) * per_iter):
                qc, kc, vc = [stage[j, pl.ds(r, length, stride=dil), :].astype(BF16) for j in range(3)]
                for blk in range(nblk):
                    rows = pl.ds(r + blk * nb * dil, nb, stride=dil)
                    q_blk = qc[blk * nb:(blk + 1) * nb]
                    if blk == 0:
                        blocks.append((rows, q_blk, kc[0:nb], vc[0:nb], causal))
                    else:
                        blocks.append((rows, q_blk, kc[(blk - 1) * nb:(blk + 1) * nb],
                                       vc[(blk - 1) * nb:(blk + 1) * nb], band))
            attend_many(gi, blocks)

    chunk = 2 * nb

    def merge_body(c, carry):
        rows = pl.ds(pl.multiple_of(c * chunk, chunk), chunk)
        la, lb, lc = lse_acc[0, rows, :], lse_acc[1, rows, :], lse_acc[2, rows, :]
        mx = jnp.maximum(jnp.maximum(la, lb), lc)
        ea, eb, ec = jnp.exp(la - mx), jnp.exp(lb - mx), jnp.exp(lc - mx)
        den = ea + eb + ec
        out = ((ea / den) * o_acc[0, rows, :] + (eb / den) * o_acc[1, rows, :]
               + (ec / den) * o_acc[2, rows, :])
        o_ref[rows, :] = out.astype(o_ref.dtype)
        return carry

    lax.fori_loop(0, seq // chunk, merge_body, 0)


def _dil_attention(qkv, batch, seq):
    m = qkv.shape[0]
    col_blocks = DIL_WIDTH // LANES
    specs = [pl.BlockSpec((seq, LANES), lambda b, h, c=(g * 3 + j) * col_blocks: (b, c + h))
             for g in range(len(DIL_CONFIGS)) for j in range(3)]
    blk = 10 * _nbytes((seq, LANES), BF16)
    scratch = 9 * _nbytes((seq, LANES), F32)
    return pl.pallas_call(
        functools.partial(_dil_attn_kernel, seq=seq, scale=DIL_DIM ** -0.5),
        grid=(batch, DIL_HEADS),
        in_specs=specs,
        out_specs=pl.BlockSpec((seq, LANES), lambda b, h: (b, h)),
        out_shape=jax.ShapeDtypeStruct((m, DIL_WIDTH), BF16),
        scratch_shapes=[pltpu.VMEM((3, seq, LANES), F32)] * 3,
        name="dil_attention",
        compiler_params=_params(("parallel", "parallel"), blk + scratch // 2),
    )(*([qkv] * 9))


POOL_HALO = 16


def _pool_kernel(cur_ref, halo_ref, w_ref, scale_ref, o_ref, buf, *, ts):
    i = pl.program_id(1)
    buf[0:POOL_HALO, :] = jnp.where(i > 0, halo_ref[...], 0.0)
    buf[POOL_HALO:, :] = cur_ref[...]
    t = i * ts + lax.broadcasted_iota(jnp.int32, (ts, 1), 0)
    for g, win in enumerate(POOL_WINDOWS):
        sl = slice(g * POOL_GROUP, (g + 1) * POOL_GROUP)
        tok = buf[POOL_HALO:POOL_HALO + ts, sl]
        acc = tok
        for j in range(1, win):
            acc = acc + buf[POOL_HALO - j:POOL_HALO - j + ts, sl]
        cnt = jnp.minimum(t + 1, win).astype(F32)
        d = (acc / cnt - tok).astype(BF16)
        o_ref[:, sl] = (_dot(d, w_ref[g]) * scale_ref[:, sl]).astype(o_ref.dtype)


def _pool_mixer(u_pool, w_grp, scale, layer, batch, seq, ts=256):
    m, c = u_pool.shape
    nt = seq // ts
    hb = ts // POOL_HALO
    blk = (_nbytes((ts, c), F32) + _nbytes((POOL_HALO, c), F32) + _nbytes(w_grp.shape[1:], BF16)
           + _nbytes((ts, c), BF16) + _nbytes((ts + POOL_HALO, c), F32))
    return pl.pallas_call(
        functools.partial(_pool_kernel, ts=ts), grid=(batch, nt),
        in_specs=[pl.BlockSpec((ts, c), lambda b, i: (b * nt + i, 0)),
                  pl.BlockSpec((POOL_HALO, c), lambda b, i: ((b * nt + i) * hb - jnp.minimum(i, 1), 0)),
                  _layer_spec(w_grp.shape[1:], layer, lambda b, i: (0, 0, 0)),
                  _layer_spec((1, c), layer, lambda b, i: (0, 0))],
        out_specs=pl.BlockSpec((ts, c), lambda b, i: (b * nt + i, 0)),
        out_shape=jax.ShapeDtypeStruct((m, c), BF16),
        scratch_shapes=[pltpu.VMEM((ts + POOL_HALO, c), F32)], name="pool_mixer",
        compiler_params=_params(("parallel", "parallel"), blk),
    )(u_pool, u_pool, w_grp, scale)


CONV_HALO = 32


SUBLANES = 8


def _conv_kernel(cur_ref, halo_ref, w_ref, b_ref, g_ref, beta_ref, o_ref, buf, shifted, hbuf, *, ts):
    i = pl.program_id(1)
    c = CONV_WIDTH

    def glu(u):
        return u[:, :c] * jax.nn.sigmoid(u[:, c:])

    buf[0:CONV_HALO, :] = jnp.where(i > 0, glu(halo_ref[...]), 0.0)
    buf[CONV_HALO:, :] = glu(cur_ref[...])
    span = ts + CONV_HALO - SUBLANES
    for r in range(1, SUBLANES):
        shifted[r - 1, 0:span, :] = buf[r:r + span, :]
    off = CONV_HALO - (CONV_K - 1)
    for cc in range(c // LANES):
        sl = slice(cc * LANES, (cc + 1) * LANES)
        acc = jnp.zeros((ts, LANES), F32)
        for k in range(CONV_K):
            r = (off + k) % SUBLANES
            base = off + k - r
            win = buf[base:base + ts, sl] if r == 0 else shifted[r - 1, base:base + ts, sl]
            acc = acc + w_ref[k:k + 1, sl] * win
        hbuf[:, sl] = acc + b_ref[:, sl]
    h = hbuf[...]
    mu = jnp.mean(h, axis=-1, keepdims=True)
    var = jnp.mean(jnp.square(h - mu), axis=-1, keepdims=True)
    y = (h - mu) * lax.rsqrt(var + LN_EPS) * g_ref[...] + beta_ref[...]
    o_ref[...] = (y * jax.nn.sigmoid(y)).astype(o_ref.dtype)


def _conv_mixer(u_conv, w_dw, b_dw, ln_g, ln_b, layer, batch, seq, ts=256):
    m, c2 = u_conv.shape
    c = c2 // 2
    nt = seq // ts
    hb = ts // CONV_HALO
    vec = _layer_spec((1, c), layer, lambda b, i: (0, 0))
    blk = (_nbytes((ts, c2), F32) + _nbytes((CONV_HALO, c2), F32) + _nbytes(w_dw.shape[1:], F32)
           + _nbytes((ts, c), BF16) + 2 * _nbytes((ts + CONV_HALO, c), F32))
    return pl.pallas_call(
        functools.partial(_conv_kernel, ts=ts), grid=(batch, nt),
        in_specs=[pl.BlockSpec((ts, c2), lambda b, i: (b * nt + i, 0)),
                  pl.BlockSpec((CONV_HALO, c2), lambda b, i: ((b * nt + i) * hb - jnp.minimum(i, 1), 0)),
                  _layer_spec(w_dw.shape[1:], layer, lambda b, i: (0, 0)), vec, vec, vec],
        out_specs=pl.BlockSpec((ts, c), lambda b, i: (b * nt + i, 0)),
        out_shape=jax.ShapeDtypeStruct((m, c), BF16),
        scratch_shapes=[pltpu.VMEM((ts + CONV_HALO, c), F32),
                        pltpu.VMEM((SUBLANES - 1, ts + CONV_HALO, c), F32), pltpu.VMEM((ts, c), F32)],
        name="conv_mixer",
        compiler_params=_params(("parallel", "parallel"), blk + 4 * _nbytes((ts + CONV_HALO, c), F32)),
    )(u_conv, u_conv, w_dw, b_dw, ln_g, ln_b)


def _merge_kernel(x_ref, ha, hb, hc, hd, g0, g1, g2, g3, pa, pb, pc, pd, bias_ref, o_ref):
    x = x_ref[...]
    acc = None
    for b, (h_ref, wg_ref, wp_ref) in enumerate(((ha, g0, pa), (hb, g1, pb), (hc, g2, pc), (hd, g3, pd))):
        gate = jax.nn.sigmoid(_dot(x, wg_ref[...]) + bias_ref[b:b + 1, :])
        term = gate * _dot(h_ref[...], wp_ref[...])
        acc = term if acc is None else acc + term
    o_ref[...] = acc.astype(o_ref.dtype)


def _merge(x_bf, hs, w_in, w_projs, b_gate, layer, tm=512, tn=512):
    m, d = x_bf.shape
    kb = hs[0].shape[1]
    nj = d // tn
    c0 = OFF_GATE // tn
    gate_spec = lambda b: _layer_spec((d, tn), layer, lambda j, i: (0, c0 + b * nj + j))
    blk = (_nbytes((tm, d), BF16) + 4 * _nbytes((tm, kb), BF16) + 4 * _nbytes((d, tn), BF16)
           + 4 * _nbytes((kb, tn), BF16) + _nbytes((tm, tn), BF16) + 4 * _nbytes((tm, tn), F32))
    return pl.pallas_call(
        _merge_kernel, grid=(nj, m // tm),
        in_specs=([pl.BlockSpec((tm, d), lambda j, i: (i, 0))]
                  + [pl.BlockSpec((tm, kb), lambda j, i: (i, 0))] * 4
                  + [gate_spec(b) for b in range(N_BRANCHES)]
                  + [_layer_spec((kb, tn), layer, lambda j, i: (0, j))] * 4
                  + [_layer_spec((N_BRANCHES, tn), layer, lambda j, i: (0, j))]),
        out_specs=pl.BlockSpec((tm, tn), lambda j, i: (i, j)),
        out_shape=jax.ShapeDtypeStruct((m, d), BF16), name="gated_merge",
        compiler_params=_params(("parallel", "parallel"), blk),
    )(x_bf, *hs, w_in, w_in, w_in, w_in, *w_projs, b_gate)


def _proj_ln_kernel(a_ref, w_ref, res_ref, g_ref, b_ref, of_ref, ob_ref, acc_ref, *, alpha):
    k = pl.program_id(1)

    @pl.when(k == 0)
    def _():
        acc_ref[...] = jnp.zeros_like(acc_ref)

    acc_ref[...] += _dot(a_ref[...], w_ref[...])

    @pl.when(k == pl.num_programs(1) - 1)
    def _():
        y = alpha * res_ref[...] + acc_ref[...]
        mu = jnp.mean(y, axis=-1, keepdims=True)
        var = jnp.mean(jnp.square(y - mu), axis=-1, keepdims=True)
        out = (y - mu) * lax.rsqrt(var + LN_EPS) * g_ref[...] + b_ref[...]
        of_ref[...] = out
        ob_ref[...] = out.astype(BF16)


def _proj_ln(a, w, res, g, b, layer, alpha, tm, tk, name):
    m, k = a.shape
    d = w.shape[2]
    vec = _layer_spec((1, d), layer, lambda i, kk: (0, 0))
    row = pl.BlockSpec((tm, d), lambda i, kk: (i, 0))
    blk = (_nbytes((tm, tk), BF16) + _nbytes((tk, d), BF16) + 2 * _nbytes((tm, d), F32)
           + _nbytes((tm, d), BF16) + _nbytes((tm, d), F32))
    return pl.pallas_call(
        functools.partial(_proj_ln_kernel, alpha=alpha), grid=(m // tm, k // tk),
        in_specs=[pl.BlockSpec((tm, tk), lambda i, kk: (i, kk)),
                  _layer_spec((tk, d), layer, lambda i, kk: (kk, 0)), row, vec, vec],
        out_specs=[row, row],
        out_shape=[jax.ShapeDtypeStruct((m, d), F32), jax.ShapeDtypeStruct((m, d), BF16)],
        scratch_shapes=[pltpu.VMEM((tm, d), F32)], name=name,
        compiler_params=_params(("parallel", "arbitrary"), blk),
    )(a, w, res, g, b)


FFN_HALO = 8


def _ffn_up_kernel(x_ref, wv_ref, wg_ref, dwv_ref, dwg_ref, bv_ref, bg_ref, o_ref,
                   wv_bf, wg_bf, bufv, bufg, *, tm, tiles_per_seq):
    i = pl.program_id(1)
    first = lax.rem(i, tiles_per_seq) == 0

    @pl.when(i == 0)
    def _():
        wv_bf[...] = wv_ref[...].astype(BF16)
        wg_bf[...] = wg_ref[...].astype(BF16)

    @pl.when(first)
    def _():
        bufv[0:FFN_HALO, :] = jnp.zeros((FFN_HALO, bufv.shape[1]), F32)
        bufg[0:FFN_HALO, :] = jnp.zeros((FFN_HALO, bufg.shape[1]), F32)

    @pl.when(jnp.logical_not(first))
    def _():
        bufv[0:FFN_HALO, :] = bufv[tm:tm + FFN_HALO, :]
        bufg[0:FFN_HALO, :] = bufg[tm:tm + FFN_HALO, :]

    def conv(buf, w_ref, b_ref, cs):
        off = FFN_HALO - (FFN_K - 1)
        acc = w_ref[0:1, cs] * buf[off:off + tm, cs]
        for k in range(1, FFN_K):
            acc = acc + w_ref[k:k + 1, cs] * buf[off + k:off + k + tm, cs]
        return acc + b_ref[:, cs]

    x = x_ref[...]
    for c in range(o_ref.shape[1] // MXU_COLS):
        cs = slice(c * MXU_COLS, (c + 1) * MXU_COLS)
        bufv[FFN_HALO:, cs] = _dot(x, wv_bf[:, cs])
        bufg[FFN_HALO:, cs] = _dot(x, wg_bf[:, cs])
        val = conv(bufv, dwv_ref, bv_ref, cs)
        gate = conv(bufg, dwg_ref, bg_ref, cs)
        o_ref[:, cs] = (val * (gate * jax.nn.sigmoid(gate))).astype(o_ref.dtype)


def _ffn_up(x_bf, w_up, dw, dw_b, layer, seq, tm=1024, tn=512):
    m, d = x_bf.shape
    f = w_up.shape[2] // 2
    nj = f // tn
    blk = (_nbytes((tm, d), BF16) + 2 * _nbytes((d, tn), F32) + _nbytes((d, tn), BF16)
           + _nbytes((tm, tn), BF16) + _nbytes((tm + FFN_HALO, tn), F32))
    return pl.pallas_call(
        functools.partial(_ffn_up_kernel, tm=tm, tiles_per_seq=seq // tm), grid=(nj, m // tm),
        in_specs=[pl.BlockSpec((tm, d), lambda j, i: (i, 0)),
                  _layer_spec((d, tn), layer, lambda j, i: (0, j)),
                  _layer_spec((d, tn), layer, lambda j, i: (0, nj + j)),
                  _layer_spec((FFN_K, tn), layer, lambda j, i: (0, j)),
                  _layer_spec((FFN_K, tn), layer, lambda j, i: (0, nj + j)),
                  _layer_spec((1, tn), layer, lambda j, i: (0, j)),
                  _layer_spec((1, tn), layer, lambda j, i: (0, nj + j))],
        out_specs=pl.BlockSpec((tm, tn), lambda j, i: (i, j)),
        out_shape=jax.ShapeDtypeStruct((m, f), BF16),
        scratch_shapes=[pltpu.VMEM((d, tn), BF16)] * 2 + [pltpu.VMEM((tm + FFN_HALO, tn), F32)] * 2,
        name="ffn_up",
        compiler_params=_params(("parallel", "arbitrary"), blk),
    )(x_bf, w_up, w_up, dw, dw, dw_b, dw_b)


def _realign_kernel(main_ref, next_ref, o_ref, *, shift):
    tn = o_ref.shape[1]
    window = jnp.concatenate([main_ref[...], next_ref[...]], axis=1)
    o_ref[...] = pltpu.roll(window, window.shape[1] - shift, 1)[:, :tn].astype(o_ref.dtype)


def _realign_tail(w_in, col0, tr=512, tn=1024):
    depth, d, n_in = w_in.shape
    n = n_in - col0
    base, shift = col0 - col0 % LANES, col0 % LANES
    b0, step = base // tn, tn // LANES
    blk = _nbytes((tr, tn + LANES), F32) + _nbytes((tr, tn), BF16)
    return pl.pallas_call(
        functools.partial(_realign_kernel, shift=shift), grid=(depth, d // tr, n // tn),
        in_specs=[pl.BlockSpec((None, tr, tn), lambda l, i, j: (l, i, b0 + j)),
                  pl.BlockSpec((None, tr, LANES), lambda l, i, j: (l, i, (b0 + j + 1) * step))],
        out_specs=pl.BlockSpec((None, tr, tn), lambda l, i, j: (l, i, j)),
        out_shape=jax.ShapeDtypeStruct((depth, d, n), BF16), name="realign_w_in",
        compiler_params=_params(("parallel", "parallel", "parallel"), blk),
    )(w_in, w_in)


def _prep_in_proj(w_in):
    pe1 = 2 * MLA_LORA + MLA_ROPE
    w_lat = jnp.pad(w_in[:, :, :pe1], ((0, 0), (0, 0), (0, LANES - MLA_ROPE))).astype(BF16)
    return w_lat, _realign_tail(w_in, pe1)


def _prep_mla_up(mla_w_uq, mla_w_ukv):
    depth = mla_w_uq.shape[0]
    w_uq = jnp.pad(mla_w_uq.reshape(depth, MLA_LORA, MLA_HEADS, MLA_QK),
                   ((0, 0), (0, 0), (0, 0), (0, 2 * LANES - MLA_QK)))
    w_uq = w_uq.reshape(depth, MLA_LORA, -1).astype(BF16)
    ukv = mla_w_ukv.reshape(depth, MLA_LORA, MLA_HEADS, MLA_NOPE + MLA_V)
    w_ukv = jnp.concatenate([ukv[..., :MLA_NOPE].reshape(depth, MLA_LORA, -1),
                             ukv[..., MLA_NOPE:].reshape(depth, MLA_LORA, -1)], axis=2).astype(BF16)
    return w_uq, w_ukv


def kernel(x, positions, w_in, b_gate, mla_gq, mla_gkv, mla_w_uq, mla_w_ukv, mla_w_proj, pool_w, pool_scale, pool_w_proj, conv_dw, conv_dw_b, conv_ln_g, conv_ln_b, conv_w_proj, dil_w_proj, mix_w_out, ln1_g, ln1_b, ffn_w_up, ffn_dw, ffn_dw_b, ffn_w_down, ln2_g, ln2_b):
    batch, seq, d_model = x.shape
    depth = w_in.shape[0]
    m = batch * seq
    alpha = (2 * depth) ** 0.25
    ffn_dim = ffn_w_down.shape[1]

    tabs = _rope_tables(positions)
    mla_tabs, dil_tabs = tabs[:3], tabs[3:]

    w_lat, w_in_r = _prep_in_proj(w_in)
    w_uq, w_ukv = _prep_mla_up(mla_w_uq, mla_w_ukv)
    w_projs = tuple(w.astype(BF16) for w in (mla_w_proj, pool_w_proj, conv_w_proj, dil_w_proj))
    w_out = mix_w_out.astype(BF16)
    w_down = ffn_w_down.astype(BF16)
    pool_w_bf = pool_w.astype(BF16)
    conv_dw_p = jnp.pad(conv_dw, ((0, 0), (0, 1), (0, 0)))
    vec = lambda v: v[:, None, :]
    gq, gkv, p_scale = vec(mla_gq), vec(mla_gkv), vec(pool_scale)
    c_b, c_g, c_beta = vec(conv_dw_b), vec(conv_ln_g), vec(conv_ln_b)
    g1, b1, g2, b2 = vec(ln1_g), vec(ln1_b), vec(ln2_g), vec(ln2_b)
    f_b = vec(ffn_dw_b)

    xf = x.reshape(m, d_model)
    xb = xf.astype(BF16)
    for l in range(depth):
        q, k, v = _latent(xb, w_lat, gq, gkv, w_uq, w_ukv, l, mla_tabs, tm=512)
        h_a = _mla_attention(q, k, v, batch, seq)

        u_pool = _in_proj(xb, w_in_r, l, OFF_POOL, POOL_WIDTH, F32, 1024, 1024, "pool_in")
        h_b = _pool_mixer(u_pool, pool_w_bf, p_scale, l, batch, seq)

        u_conv = _in_proj(xb, w_in_r, l, OFF_CONV, 2 * CONV_WIDTH, F32, 1024, 1024, "conv_in")
        h_c = _conv_mixer(u_conv, conv_dw_p, c_b, c_g, c_beta, l, batch, seq)

        qkv = _dil_proj(xb, w_in_r, l, dil_tabs, tm=1024)
        h_d = _dil_attention(qkv, batch, seq)

        merged = _merge(xb, (h_a, h_b, h_c, h_d), w_in_r, w_projs, b_gate, l)
        xf, xb = _proj_ln(merged, w_out, xf, g1, b1, l, alpha, tm=512, tk=d_model, name="mix_out_ln1")

        hid = _ffn_up(xb, ffn_w_up, ffn_dw, f_b, l, seq)
        xf, xb = _proj_ln(hid, w_down, xf, g2, b2, l, alpha, tm=512, tk=ffn_dim // 4, name="ffn_down_ln2")
    return xf.reshape(batch, seq, d_model)
```

```python
import functools

import jax
import jax.numpy as jnp
from jax import lax
from jax.experimental import pallas as pl
from jax.experimental.pallas import tpu as pltpu

MLA_HEADS = 8
MLA_LORA = 512
MLA_NOPE = 128
MLA_ROPE = 64
MLA_V = 128
MLA_QK = MLA_NOPE + MLA_ROPE
POOL_WINDOWS = (2, 4, 8, 16)
POOL_GROUP = 256
POOL_WIDTH = 1024
CONV_WIDTH = 1024
CONV_K = 31
DIL_CONFIGS = ((128, 1), (512, 4), (2048, 16))
DIL_HEADS = 8
DIL_DIM = 128
DIL_WIDTH = DIL_HEADS * DIL_DIM
DIL_ROT = DIL_DIM // 4
DIL_BLOCK = 128
N_BRANCHES = 4
ROPE_THETA = 500000.0
FFN_K = 3
LN_EPS = 1e-5
RMS_EPS = 1e-6
NEG_INF = -1e30
LOG2_E = 1.4426950408889634

OFF_POOL = 0
OFF_CONV = OFF_POOL + POOL_WIDTH
OFF_DIL = OFF_CONV + 2 * CONV_WIDTH
OFF_GATE = OFF_DIL + len(DIL_CONFIGS) * 3 * DIL_WIDTH

LANES = 128
V7X_VMEM_BYTES = 64 * 1024 * 1024
VMEM_CAP = V7X_VMEM_BYTES - 8 * 1024 * 1024

BF16 = jnp.bfloat16
F32 = jnp.float32


def _params(semantics, block_bytes, flags=None):
    limit = min(VMEM_CAP, max(32 * 1024 * 1024, 2 * block_bytes + 16 * 1024 * 1024))
    return pltpu.CompilerParams(dimension_semantics=semantics, vmem_limit_bytes=limit, flags=flags)


def _nbytes(shape, dtype):
    n = 1
    for s in shape:
        n *= s
    return n * jnp.dtype(dtype).itemsize


def _dot(a, b):
    return jnp.dot(a, b, preferred_element_type=F32)


def _dot_nt(a, b):
    return lax.dot_general(a, b, (((1,), (1,)), ((), ())), preferred_element_type=F32)


def _layer_spec(block, layer, index_map):
    return pl.BlockSpec((None,) + tuple(block), lambda *g: (layer,) + tuple(index_map(*g)))


def _rope_lanes(x, cos, sin_lo, sin_hi, half):
    return (x * cos + pltpu.roll(x, LANES - half, 1) * sin_lo
            + pltpu.roll(x, half, 1) * sin_hi)


def _rope_table_kernel(pos_ref, c_ref, cos_a, lo_a, hi_a, cos_b, lo_b, hi_b):
    pos = pos_ref[...]
    for base, (co, lo, hi) in ((0, (cos_a, lo_a, hi_a)), (3, (cos_b, lo_b, hi_b))):
        ang = pos * c_ref[base:base + 1, :]
        sn = jnp.sin(ang)
        co[...] = jnp.cos(ang)
        lo[...] = sn * c_ref[base + 1:base + 2, :]
        hi[...] = sn * c_ref[base + 2:base + 3, :]


def _lane_pattern(rot_dim):
    half = rot_dim // 2
    inv_freq = ROPE_THETA ** (-jnp.arange(half, dtype=F32) * 2.0 / rot_dim)
    lane = jnp.arange(LANES)
    freq = jnp.where(lane < rot_dim, jnp.tile(inv_freq, LANES // half), 0.0)
    lo = jnp.where(lane < half, -1.0, 0.0)
    hi = jnp.where((lane >= half) & (lane < rot_dim), 1.0, 0.0)
    return [freq.astype(F32), lo.astype(F32), hi.astype(F32)]


def _rope_tables(positions):
    m = positions.size
    pos = jnp.broadcast_to(positions.reshape(m, 1).astype(F32), (m, LANES))
    consts = jnp.stack(_lane_pattern(MLA_ROPE) + _lane_pattern(DIL_ROT)
                       + [jnp.zeros((LANES,), F32)] * 2)
    tm = 1024
    spec = pl.BlockSpec((tm, LANES), lambda i: (i, 0))
    out = jax.ShapeDtypeStruct((m, LANES), F32)
    return pl.pallas_call(
        _rope_table_kernel, grid=(m // tm,),
        in_specs=[spec, pl.BlockSpec((8, LANES), lambda i: (0, 0))],
        out_specs=[spec] * 6, out_shape=[out] * 6, name="rope_tables",
        compiler_params=_params(("parallel",), 7 * _nbytes((tm, LANES), F32)),
    )(pos, consts)


def _mm_kernel(a_ref, w_ref, o_ref):
    o_ref[...] = _dot(a_ref[...], w_ref[...]).astype(o_ref.dtype)


def _in_proj(a, w_in, layer, col_off, n, out_dtype, tm, tn, name):
    m, k = a.shape
    c0 = col_off // tn
    blk = (_nbytes((tm, k), a.dtype) + _nbytes((k, tn), w_in.dtype) + _nbytes((tm, tn), out_dtype))
    return pl.pallas_call(
        _mm_kernel, grid=(n // tn, m // tm),
        in_specs=[pl.BlockSpec((tm, k), lambda j, i: (i, 0)),
                  _layer_spec((k, tn), layer, lambda j, i: (0, c0 + j))],
        out_specs=pl.BlockSpec((tm, tn), lambda j, i: (i, j)),
        out_shape=jax.ShapeDtypeStruct((m, n), out_dtype), name=name,
        compiler_params=_params(("parallel", "parallel"), blk),
    )(a, w_in)


MXU_COLS = 256


def _skewed(i, nt, bufs, compute_chunks, epilogue_chunks):
    def run(cur, prv, do_compute, do_epilogue):
        cs = compute_chunks(cur, prv) if do_compute else []
        es = epilogue_chunks(prv) if do_epilogue else []
        per = -(-len(es) // len(cs)) if cs else len(es)
        for k in range(max(len(cs), 1)):
            if cs:
                cs[k]()
            for e in es[k * per:(k + 1) * per]:
                e()

    parity = lax.rem(i, 2)
    middle = (i > 0) & (i < nt)
    pl.when(i == 0)(lambda: run(bufs[0], None, True, False))
    pl.when(middle & (parity == 0))(lambda: run(bufs[0], bufs[1], True, True))
    pl.when(middle & (parity == 1))(lambda: run(bufs[1], bufs[0], True, True))
    pl.when(i == nt)(lambda: run(None, bufs[1 - nt % 2], False, True))


def _skewed_in(tm_width, nt):
    return pl.BlockSpec(tm_width, lambda j, i: (jnp.minimum(i, nt - 1), 0))


def _dil_proj_kernel(a_ref, w_ref, cos_ref, lo_ref, hi_ref, o_ref, acc_a, acc_b, *, nt):
    tm = o_ref.shape[0]
    is_v = lax.rem(pl.program_id(0), 3) == 2

    def compute_chunks(cur, prv):
        def chunk(c):
            cs = slice(c * MXU_COLS, (c + 1) * MXU_COLS)
            cur[:, cs] = _dot(a_ref[...], w_ref[:, cs])
        return [functools.partial(chunk, c) for c in range(DIL_WIDTH // MXU_COLS)]

    def epilogue_chunks(prv):
        half = tm // 2

        def chunk(h, r):
            rows = slice(r * half, (r + 1) * half)
            sl = slice(h * DIL_DIM, (h + 1) * DIL_DIM)
            cos = jnp.where(is_v, 1.0, cos_ref[rows, :])
            lo = jnp.where(is_v, 0.0, lo_ref[rows, :])
            hi = jnp.where(is_v, 0.0, hi_ref[rows, :])
            o_ref[rows, sl] = _rope_lanes(prv[rows, sl], cos, lo, hi, DIL_ROT // 2).astype(o_ref.dtype)
        return [functools.partial(chunk, h, r) for h in range(DIL_HEADS) for r in range(2)]

    _skewed(pl.program_id(1), nt, (acc_a, acc_b), compute_chunks, epilogue_chunks)


def _dil_proj(x_bf, w_in, layer, tabs, tm):
    m, k = x_bf.shape
    n = len(DIL_CONFIGS) * 3 * DIL_WIDTH
    tn = DIL_WIDTH
    nt = m // tm
    c0 = OFF_DIL // tn
    prev_row = lambda j, i: (jnp.maximum(i - 1, 0), 0)
    tab_spec = pl.BlockSpec((tm, LANES), prev_row)
    blk = (_nbytes((tm, k), BF16) + _nbytes((k, tn), BF16) + _nbytes((tm, tn), BF16)
           + 3 * _nbytes((tm, LANES), F32) + _nbytes((tm, tn), F32))
    return pl.pallas_call(
        functools.partial(_dil_proj_kernel, nt=nt), grid=(n // tn, nt + 1),
        in_specs=[_skewed_in((tm, k), nt),
                  _layer_spec((k, tn), layer, lambda j, i: (0, c0 + j)),
                  tab_spec, tab_spec, tab_spec],
        out_specs=pl.BlockSpec((tm, tn), lambda j, i: (jnp.maximum(i - 1, 0), j)),
        out_shape=jax.ShapeDtypeStruct((m, n), BF16),
        scratch_shapes=[pltpu.VMEM((tm, tn), F32)] * 2, name="dil_proj",
        compiler_params=_params(("parallel", "arbitrary"), blk),
    )(x_bf, w_in, *tabs)


def _latent_kernel(x_ref, wlat_ref, gq_ref, gkv_ref, wuq_ref, wukv_ref,
                   cos_ref, lo_ref, hi_ref, q_ref, k_ref, v_ref):
    lat = _dot(x_ref[...], wlat_ref[...])
    cos, lo, hi = cos_ref[...], lo_ref[...], hi_ref[...]

    def rms(c, g):
        return c * lax.rsqrt(jnp.mean(jnp.square(c), axis=-1, keepdims=True) + RMS_EPS) * g

    def rope(v):
        return _rope_lanes(v, cos, lo, hi, MLA_ROPE // 2)

    cq = rms(lat[:, :MLA_LORA], gq_ref[...]).astype(BF16)
    ckv = rms(lat[:, MLA_LORA:2 * MLA_LORA], gkv_ref[...]).astype(BF16)
    kpe = rope(lat[:, 2 * MLA_LORA:]).astype(BF16)
    q = _dot(cq, wuq_ref[...])
    kv = _dot(ckv, wukv_ref[...])
    for h in range(MLA_HEADS):
        b0 = h * 2 * LANES
        q_ref[:, b0:b0 + LANES] = q[:, b0:b0 + LANES].astype(BF16)
        q_ref[:, b0 + LANES:b0 + 2 * LANES] = rope(q[:, b0 + LANES:b0 + 2 * LANES]).astype(BF16)
        k_ref[:, b0:b0 + LANES] = kv[:, h * LANES:(h + 1) * LANES].astype(BF16)
        k_ref[:, b0 + LANES:b0 + 2 * LANES] = kpe
    v_ref[...] = kv[:, MLA_HEADS * MLA_NOPE:].astype(BF16)


def _latent(x_bf, w_lat, gq, gkv, w_uq, w_ukv, layer, tabs, tm):
    m, k = x_bf.shape
    nl = w_lat.shape[2]
    hq = MLA_HEADS * 2 * LANES
    hv = MLA_HEADS * MLA_V
    zero2 = lambda i: (0, 0)
    row = lambda w: pl.BlockSpec((tm, w), lambda i: (i, 0))
    blk = (_nbytes((tm, k), BF16) + _nbytes((k, nl), BF16) + 2 * _nbytes((MLA_LORA, hq), BF16)
           + 2 * _nbytes((tm, hq), BF16) + _nbytes((tm, hv), BF16) + 3 * _nbytes((tm, LANES), F32))
    return pl.pallas_call(
        _latent_kernel, grid=(m // tm,),
        in_specs=[row(k), _layer_spec((k, nl), layer, zero2),
                  _layer_spec((1, MLA_LORA), layer, zero2), _layer_spec((1, MLA_LORA), layer, zero2),
                  _layer_spec((MLA_LORA, hq), layer, zero2), _layer_spec((MLA_LORA, hq), layer, zero2),
                  row(LANES), row(LANES), row(LANES)],
        out_specs=[row(hq), row(hq), row(hv)],
        out_shape=[jax.ShapeDtypeStruct((m, hq), BF16), jax.ShapeDtypeStruct((m, hq), BF16),
                   jax.ShapeDtypeStruct((m, hv), BF16)],
        name="mla_latent",
        compiler_params=_params(("parallel",), blk),
    )(x_bf, w_lat, gq, gkv, w_uq, w_ukv, *tabs)


def _mla_attn_kernel(q_ref, k_ref, v_ref, o_ref, *, tq, scale):
    s = q_ref.shape[0]
    c = scale * LOG2_E
    diag = (lax.broadcasted_iota(jnp.int32, (tq, tq), 1)
            <= lax.broadcasted_iota(jnp.int32, (tq, tq), 0))
    for qi in range(s // tq):
        s0, s1 = qi * tq, (qi + 1) * tq
        q = q_ref[s0:s1, :]
        sd = jnp.where(diag, _dot_nt(q, k_ref[s0:s1, :]), NEG_INF)
        mx = jnp.max(sd, axis=-1, keepdims=True)
        if qi > 0:
            so = _dot_nt(q, k_ref[:s0, :])
            mx = jnp.maximum(mx, jnp.max(so, axis=-1, keepdims=True))
            po = jnp.exp2((so - mx) * c)
        pd = jnp.exp2((sd - mx) * c)
        den = jnp.sum(pd, axis=-1, keepdims=True)
        o = _dot(pd.astype(BF16), v_ref[s0:s1, :])
        if qi > 0:
            den = den + jnp.sum(po, axis=-1, keepdims=True)
            o = o + _dot(po.astype(BF16), v_ref[:s0, :])
        o_ref[s0:s1, :] = (o / den).astype(o_ref.dtype)


def _mla_attention(q, k, v, batch, seq, tq=256):
    m = q.shape[0]
    dq = 2 * LANES
    blk = 2 * _nbytes((seq, dq), BF16) + 2 * _nbytes((seq, MLA_V), BF16)
    return pl.pallas_call(
        functools.partial(_mla_attn_kernel, tq=tq, scale=MLA_QK ** -0.5),
        grid=(batch, MLA_HEADS),
        in_specs=[pl.BlockSpec((seq, dq), lambda b, h: (b, h)),
                  pl.BlockSpec((seq, dq), lambda b, h: (b, h)),
                  pl.BlockSpec((seq, MLA_V), lambda b, h: (b, h))],
        out_specs=pl.BlockSpec((seq, MLA_V), lambda b, h: (b, h)),
        out_shape=jax.ShapeDtypeStruct((m, MLA_HEADS * MLA_V), BF16), name="mla_attention",
        compiler_params=_params(("parallel", "parallel"), blk + 4 * _nbytes((tq, seq), F32)),
    )(q, k, v)


DIL_UNROLL = 8


def _dil_attn_kernel(q1, k1, v1, q2, k2, v2, q3, k3, v3, o_ref, stage, o_acc, lse_acc, *, seq, scale):
    nb = DIL_BLOCK
    diff = (lax.broadcasted_iota(jnp.int32, (nb, 2 * nb), 0)
            - lax.broadcasted_iota(jnp.int32, (nb, 2 * nb), 1))
    band = (diff + nb >= 0) & (diff <= 0)
    causal = (lax.broadcasted_iota(jnp.int32, (nb, nb), 1)
              <= lax.broadcasted_iota(jnp.int32, (nb, nb), 0))

    def attend_many(gi, blocks):
        scs = [jnp.where(mask, _dot_nt(q, k), NEG_INF) for _, q, k, _, mask in blocks]
        mxs = [jnp.max(sc, axis=-1, keepdims=True) for sc in scs]
        es = [jnp.exp2((sc - mx) * (scale * LOG2_E)) for sc, mx in zip(scs, mxs)]
        dens = [jnp.sum(e, axis=-1, keepdims=True) for e in es]
        ps = [(e / den).astype(BF16) for e, den in zip(es, dens)]
        outs = [_dot(p, blk[3]) for p, blk in zip(ps, blocks)]
        for blk, o, mx, den in zip(blocks, outs, mxs, dens):
            o_acc[gi, blk[0], :] = o
            lse_acc[gi, blk[0], :] = jnp.broadcast_to(mx * scale + jnp.log(den), (nb, LANES))

    def block_body(it, carry):
        blocks = []
        for u in range(DIL_UNROLL):
            blk = it * DIL_UNROLL + u
            q0 = pl.multiple_of(blk * nb, nb)
            if u == 0:
                k0 = pl.multiple_of(jnp.maximum(blk - 1, 0) * nb, nb)
                rel = diff + (q0 - k0)
                mask = (rel >= 0) & (rel <= nb)
            else:
                k0 = pl.multiple_of(blk * nb - nb, nb)
                mask = band
            blocks.append((pl.ds(q0, nb), q1[pl.ds(q0, nb), :], k1[pl.ds(k0, 2 * nb), :],
                           v1[pl.ds(k0, 2 * nb), :], mask))
        attend_many(0, blocks)
        return carry

    lax.fori_loop(0, seq // (nb * DIL_UNROLL), block_body, 0)

    for gi, (q_ref, k_ref, v_ref) in ((1, (q2, k2, v2)), (2, (q3, k3, v3))):
        dil = DIL_CONFIGS[gi][1]
        length = seq // dil
        nblk = length // nb
        per_iter = DIL_UNROLL // nblk
        stage[0] = q_ref[...].astype(F32)
        stage[1] = k_ref[...].astype(F32)
        stage[2] = v_ref[...].astype(F32)

        for it in range(dil // per_iter):
            blocks = []
            for r in range(it * per_iter, (it + 1) * per_iter):
                qc, kc, vc = [stage[j, pl.ds(r, length, stride=dil), :].astype(BF16) for j in range(3)]
                for blk in range(nblk):
                    rows = pl.ds(r + blk * nb * dil, nb, stride=dil)
                    q_blk = qc[blk * nb:(blk + 1) * nb]
                    if blk == 0:
                        blocks.append((rows, q_blk, kc[0:nb], vc[0:nb], causal))
                    else:
                        blocks.append((rows, q_blk, kc[(blk - 1) * nb:(blk + 1) * nb],
                                       vc[(blk - 1) * nb:(blk + 1) * nb], band))
            attend_many(gi, blocks)

    chunk = 2 * nb

    def merge_body(c, carry):
        rows = pl.ds(pl.multiple_of(c * chunk, chunk), chunk)
        la, lb, lc = lse_acc[0, rows, :], lse_acc[1, rows, :], lse_acc[2, rows, :]
        mx = jnp.maximum(jnp.maximum(la, lb), lc)
        ea, eb, ec = jnp.exp(la - mx), jnp.exp(lb - mx), jnp.exp(lc - mx)
        den = ea + eb + ec
        out = ((ea / den) * o_acc[0, rows, :] + (eb / den) * o_acc[1, rows, :]
               + (ec / den) * o_acc[2, rows, :])
        o_ref[rows, :] = out.astype(o_ref.dtype)
        return carry

    lax.fori_loop(0, seq // chunk, merge_body, 0)


def _dil_attention(qkv, batch, seq):
    m = qkv.shape[0]
    col_blocks = DIL_WIDTH // LANES
    specs = [pl.BlockSpec((seq, LANES), lambda b, h, c=(g * 3 + j) * col_blocks: (b, c + h))
             for g in range(len(DIL_CONFIGS)) for j in range(3)]
    blk = 10 * _nbytes((seq, LANES), BF16)
    scratch = 9 * _nbytes((seq, LANES), F32)
    return pl.pallas_call(
        functools.partial(_dil_attn_kernel, seq=seq, scale=DIL_DIM ** -0.5),
        grid=(batch, DIL_HEADS),
        in_specs=specs,
        out_specs=pl.BlockSpec((seq, LANES), lambda b, h: (b, h)),
        out_shape=jax.ShapeDtypeStruct((m, DIL_WIDTH), BF16),
        scratch_shapes=[pltpu.VMEM((3, seq, LANES), F32)] * 3,
        name="dil_attention",
        compiler_params=_params(("parallel", "parallel"), blk + scratch // 2),
    )(*([qkv] * 9))


POOL_HALO = 16


def _pool_kernel(cur_ref, halo_ref, w_ref, scale_ref, o_ref, buf, *, ts):
    i = pl.program_id(1)
    buf[0:POOL_HALO, :] = jnp.where(i > 0, halo_ref[...], 0.0)
    buf[POOL_HALO:, :] = cur_ref[...]
    t = i * ts + lax.broadcasted_iota(jnp.int32, (ts, 1), 0)
    for g, win in enumerate(POOL_WINDOWS):
        sl = slice(g * POOL_GROUP, (g + 1) * POOL_GROUP)
        tok = buf[POOL_HALO:POOL_HALO + ts, sl]
        acc = tok
        for j in range(1, win):
            acc = acc + buf[POOL_HALO - j:POOL_HALO - j + ts, sl]
        cnt = jnp.minimum(t + 1, win).astype(F32)
        d = (acc / cnt - tok).astype(BF16)
        o_ref[:, sl] = (_dot(d, w_ref[g]) * scale_ref[:, sl]).astype(o_ref.dtype)


def _pool_mixer(u_pool, w_grp, scale, layer, batch, seq, ts=256):
    m, c = u_pool.shape
    nt = seq // ts
    hb = ts // POOL_HALO
    blk = (_nbytes((ts, c), F32) + _nbytes((POOL_HALO, c), F32) + _nbytes(w_grp.shape[1:], BF16)
           + _nbytes((ts, c), BF16) + _nbytes((ts + POOL_HALO, c), F32))
    return pl.pallas_call(
        functools.partial(_pool_kernel, ts=ts), grid=(batch, nt),
        in_specs=[pl.BlockSpec((ts, c), lambda b, i: (b * nt + i, 0)),
                  pl.BlockSpec((POOL_HALO, c), lambda b, i: ((b * nt + i) * hb - jnp.minimum(i, 1), 0)),
                  _layer_spec(w_grp.shape[1:], layer, lambda b, i: (0, 0, 0)),
                  _layer_spec((1, c), layer, lambda b, i: (0, 0))],
        out_specs=pl.BlockSpec((ts, c), lambda b, i: (b * nt + i, 0)),
        out_shape=jax.ShapeDtypeStruct((m, c), BF16),
        scratch_shapes=[pltpu.VMEM((ts + POOL_HALO, c), F32)], name="pool_mixer",
        compiler_params=_params(("parallel", "parallel"), blk),
    )(u_pool, u_pool, w_grp, scale)


CONV_HALO = 32


SUBLANES = 8


def _conv_kernel(cur_ref, halo_ref, w_ref, b_ref, g_ref, beta_ref, o_ref, buf, shifted, hbuf, *, ts):
    i = pl.program_id(1)
    c = CONV_WIDTH

    def glu(u):
        return u[:, :c] * jax.nn.sigmoid(u[:, c:])

    buf[0:CONV_HALO, :] = jnp.where(i > 0, glu(halo_ref[...]), 0.0)
    buf[CONV_HALO:, :] = glu(cur_ref[...])
    span = ts + CONV_HALO - SUBLANES
    for r in range(1, SUBLANES):
        shifted[r - 1, 0:span, :] = buf[r:r + span, :]
    off = CONV_HALO - (CONV_K - 1)
    for cc in range(c // LANES):
        sl = slice(cc * LANES, (cc + 1) * LANES)
        acc = jnp.zeros((ts, LANES), F32)
        for k in range(CONV_K):
            r = (off + k) % SUBLANES
            base = off + k - r
            win = buf[base:base + ts, sl] if r == 0 else shifted[r - 1, base:base + ts, sl]
            acc = acc + w_ref[k:k + 1, sl] * win
        hbuf[:, sl] = acc + b_ref[:, sl]
    h = hbuf[...]
    mu = jnp.mean(h, axis=-1, keepdims=True)
    var = jnp.mean(jnp.square(h - mu), axis=-1, keepdims=True)
    y = (h - mu) * lax.rsqrt(var + LN_EPS) * g_ref[...] + beta_ref[...]
    o_ref[...] = (y * jax.nn.sigmoid(y)).astype(o_ref.dtype)


def _conv_mixer(u_conv, w_dw, b_dw, ln_g, ln_b, layer, batch, seq, ts=256):
    m, c2 = u_conv.shape
    c = c2 // 2
    nt = seq // ts
    hb = ts // CONV_HALO
    vec = _layer_spec((1, c), layer, lambda b, i: (0, 0))
    blk = (_nbytes((ts, c2), F32) + _nbytes((CONV_HALO, c2), F32) + _nbytes(w_dw.shape[1:], F32)
           + _nbytes((ts, c), BF16) + 2 * _nbytes((ts + CONV_HALO, c), F32))
    return pl.pallas_call(
        functools.partial(_conv_kernel, ts=ts), grid=(batch, nt),
        in_specs=[pl.BlockSpec((ts, c2), lambda b, i: (b * nt + i, 0)),
                  pl.BlockSpec((CONV_HALO, c2), lambda b, i: ((b * nt + i) * hb - jnp.minimum(i, 1), 0)),
                  _layer_spec(w_dw.shape[1:], layer, lambda b, i: (0, 0)), vec, vec, vec],
        out_specs=pl.BlockSpec((ts, c), lambda b, i: (b * nt + i, 0)),
        out_shape=jax.ShapeDtypeStruct((m, c), BF16),
        scratch_shapes=[pltpu.VMEM((ts + CONV_HALO, c), F32),
                        pltpu.VMEM((SUBLANES - 1, ts + CONV_HALO, c), F32), pltpu.VMEM((ts, c), F32)],
        name="conv_mixer",
        compiler_params=_params(("parallel", "parallel"), blk + 4 * _nbytes((ts + CONV_HALO, c), F32)),
    )(u_conv, u_conv, w_dw, b_dw, ln_g, ln_b)


def _merge_kernel(x_ref, ha, hb, hc, hd, g0, g1, g2, g3, pa, pb, pc, pd, bias_ref, o_ref):
    x = x_ref[...]
    acc = None
    for b, (h_ref, wg_ref, wp_ref) in enumerate(((ha, g0, pa), (hb, g1, pb), (hc, g2, pc), (hd, g3, pd))):
        gate = jax.nn.sigmoid(_dot(x, wg_ref[...]) + bias_ref[b:b + 1, :])
        term = gate * _dot(h_ref[...], wp_ref[...])
        acc = term if acc is None else acc + term
    o_ref[...] = acc.astype(o_ref.dtype)


def _merge(x_bf, hs, w_in, w_projs, b_gate, layer, tm=512, tn=512):
    m, d = x_bf.shape
    kb = hs[0].shape[1]
    nj = d // tn
    c0 = OFF_GATE // tn
    gate_spec = lambda b: _layer_spec((d, tn), layer, lambda j, i: (0, c0 + b * nj + j))
    blk = (_nbytes((tm, d), BF16) + 4 * _nbytes((tm, kb), BF16) + 4 * _nbytes((d, tn), BF16)
           + 4 * _nbytes((kb, tn), BF16) + _nbytes((tm, tn), BF16) + 4 * _nbytes((tm, tn), F32))
    return pl.pallas_call(
        _merge_kernel, grid=(nj, m // tm),
        in_specs=([pl.BlockSpec((tm, d), lambda j, i: (i, 0))]
                  + [pl.BlockSpec((tm, kb), lambda j, i: (i, 0))] * 4
                  + [gate_spec(b) for b in range(N_BRANCHES)]
                  + [_layer_spec((kb, tn), layer, lambda j, i: (0, j))] * 4
                  + [_layer_spec((N_BRANCHES, tn), layer, lambda j, i: (0, j))]),
        out_specs=pl.BlockSpec((tm, tn), lambda j, i: (i, j)),
        out_shape=jax.ShapeDtypeStruct((m, d), BF16), name="gated_merge",
        compiler_params=_params(("parallel", "parallel"), blk),
    )(x_bf, *hs, w_in, w_in, w_in, w_in, *w_projs, b_gate)


def _proj_ln_kernel(a_ref, w_ref, res_ref, g_ref, b_ref, of_ref, ob_ref, acc_ref, *, alpha):
    k = pl.program_id(1)

    @pl.when(k == 0)
    def _():
        acc_ref[...] = jnp.zeros_like(acc_ref)

    acc_ref[...] += _dot(a_ref[...], w_ref[...])

    @pl.when(k == pl.num_programs(1) - 1)
    def _():
        y = alpha * res_ref[...] + acc_ref[...]
        mu = jnp.mean(y, axis=-1, keepdims=True)
        var = jnp.mean(jnp.square(y - mu), axis=-1, keepdims=True)
        out = (y - mu) * lax.rsqrt(var + LN_EPS) * g_ref[...] + b_ref[...]
        of_ref[...] = out
        ob_ref[...] = out.astype(BF16)


def _proj_ln(a, w, res, g, b, layer, alpha, tm, tk, name):
    m, k = a.shape
    d = w.shape[2]
    vec = _layer_spec((1, d), layer, lambda i, kk: (0, 0))
    row = pl.BlockSpec((tm, d), lambda i, kk: (i, 0))
    blk = (_nbytes((tm, tk), BF16) + _nbytes((tk, d), BF16) + 2 * _nbytes((tm, d), F32)
           + _nbytes((tm, d), BF16) + _nbytes((tm, d), F32))
    return pl.pallas_call(
        functools.partial(_proj_ln_kernel, alpha=alpha), grid=(m // tm, k // tk),
        in_specs=[pl.BlockSpec((tm, tk), lambda i, kk: (i, kk)),
                  _layer_spec((tk, d), layer, lambda i, kk: (kk, 0)), row, vec, vec],
        out_specs=[row, row],
        out_shape=[jax.ShapeDtypeStruct((m, d), F32), jax.ShapeDtypeStruct((m, d), BF16)],
        scratch_shapes=[pltpu.VMEM((tm, d), F32)], name=name,
        compiler_params=_params(("parallel", "arbitrary"), blk),
    )(a, w, res, g, b)


FFN_HALO = 8


def _ffn_up_kernel(x_ref, wv_ref, wg_ref, dwv_ref, dwg_ref, bv_ref, bg_ref, o_ref,
                   wv_bf, wg_bf, bufv, bufg, *, tm, tiles_per_seq):
    i = pl.program_id(1)
    first = lax.rem(i, tiles_per_seq) == 0

    @pl.when(i == 0)
    def _():
        wv_bf[...] = wv_ref[...].astype(BF16)
        wg_bf[...] = wg_ref[...].astype(BF16)

    @pl.when(first)
    def _():
        bufv[0:FFN_HALO, :] = jnp.zeros((FFN_HALO, bufv.shape[1]), F32)
        bufg[0:FFN_HALO, :] = jnp.zeros((FFN_HALO, bufg.shape[1]), F32)

    @pl.when(jnp.logical_not(first))
    def _():
        bufv[0:FFN_HALO, :] = bufv[tm:tm + FFN_HALO, :]
        bufg[0:FFN_HALO, :] = bufg[tm:tm + FFN_HALO, :]

    def conv(buf, w_ref, b_ref, cs):
        off = FFN_HALO - (FFN_K - 1)
        acc = w_ref[0:1, cs] * buf[off:off + tm, cs]
        for k in range(1, FFN_K):
            acc = acc + w_ref[k:k + 1, cs] * buf[off + k:off + k + tm, cs]
        return acc + b_ref[:, cs]

    x = x_ref[...]
    for c in range(o_ref.shape[1] // MXU_COLS):
        cs = slice(c * MXU_COLS, (c + 1) * MXU_COLS)
        bufv[FFN_HALO:, cs] = _dot(x, wv_bf[:, cs])
        bufg[FFN_HALO:, cs] = _dot(x, wg_bf[:, cs])
        val = conv(bufv, dwv_ref, bv_ref, cs)
        gate = conv(bufg, dwg_ref, bg_ref, cs)
        o_ref[:, cs] = (val * (gate * jax.nn.sigmoid(gate))).astype(o_ref.dtype)


def _ffn_up(x_bf, w_up, dw, dw_b, layer, seq, tm=1024, tn=512):
    m, d = x_bf.shape
    f = w_up.shape[2] // 2
    nj = f // tn
    blk = (_nbytes((tm, d), BF16) + 2 * _nbytes((d, tn), F32) + _nbytes((d, tn), BF16)
           + _nbytes((tm, tn), BF16) + _nbytes((tm + FFN_HALO, tn), F32))
    return pl.pallas_call(
        functools.partial(_ffn_up_kernel, tm=tm, tiles_per_seq=seq // tm), grid=(nj, m // tm),
        in_specs=[pl.BlockSpec((tm, d), lambda j, i: (i, 0)),
                  _layer_spec((d, tn), layer, lambda j, i: (0, j)),
                  _layer_spec((d, tn), layer, lambda j, i: (0, nj + j)),
                  _layer_spec((FFN_K, tn), layer, lambda j, i: (0, j)),
                  _layer_spec((FFN_K, tn), layer, lambda j, i: (0, nj + j)),
                  _layer_spec((1, tn), layer, lambda j, i: (0, j)),
                  _layer_spec((1, tn), layer, lambda j, i: (0, nj + j))],
        out_specs=pl.BlockSpec((tm, tn), lambda j, i: (i, j)),
        out_shape=jax.ShapeDtypeStruct((m, f), BF16),
        scratch_shapes=[pltpu.VMEM((d, tn), BF16)] * 2 + [pltpu.VMEM((tm + FFN_HALO, tn), F32)] * 2,
        name="ffn_up",
        compiler_params=_params(("parallel", "arbitrary"), blk),
    )(x_bf, w_up, w_up, dw, dw, dw_b, dw_b)


def _prep_in_proj(w_in):
    pe1 = 2 * MLA_LORA + MLA_ROPE
    w_lat = jnp.pad(w_in[:, :, :pe1], ((0, 0), (0, 0), (0, LANES - MLA_ROPE))).astype(BF16)
    return w_lat, w_in.astype(BF16)[:, :, pe1:]


def _prep_mla_up(mla_w_uq, mla_w_ukv):
    depth = mla_w_uq.shape[0]
    w_uq = jnp.pad(mla_w_uq.reshape(depth, MLA_LORA, MLA_HEADS, MLA_QK),
                   ((0, 0), (0, 0), (0, 0), (0, 2 * LANES - MLA_QK)))
    w_uq = w_uq.reshape(depth, MLA_LORA, -1).astype(BF16)
    ukv = mla_w_ukv.reshape(depth, MLA_LORA, MLA_HEADS, MLA_NOPE + MLA_V)
    w_ukv = jnp.concatenate([ukv[..., :MLA_NOPE].reshape(depth, MLA_LORA, -1),
                             ukv[..., MLA_NOPE:].reshape(depth, MLA_LORA, -1)], axis=2).astype(BF16)
    return w_uq, w_ukv


def kernel(x, positions, w_in, b_gate, mla_gq, mla_gkv, mla_w_uq, mla_w_ukv, mla_w_proj, pool_w, pool_scale, pool_w_proj, conv_dw, conv_dw_b, conv_ln_g, conv_ln_b, conv_w_proj, dil_w_proj, mix_w_out, ln1_g, ln1_b, ffn_w_up, ffn_dw, ffn_dw_b, ffn_w_down, ln2_g, ln2_b):
    batch, seq, d_model = x.shape
    depth = w_in.shape[0]
    m = batch * seq
    alpha = (2 * depth) ** 0.25
    ffn_dim = ffn_w_down.shape[1]

    tabs = _rope_tables(positions)
    mla_tabs, dil_tabs = tabs[:3], tabs[3:]

    w_lat, w_in_r = _prep_in_proj(w_in)
    w_uq, w_ukv = _prep_mla_up(mla_w_uq, mla_w_ukv)
    w_projs = tuple(w.astype(BF16) for w in (mla_w_proj, pool_w_proj, conv_w_proj, dil_w_proj))
    w_out = mix_w_out.astype(BF16)
    w_down = ffn_w_down.astype(BF16)
    pool_w_bf = pool_w.astype(BF16)
    conv_dw_p = jnp.pad(conv_dw, ((0, 0), (0, 1), (0, 0)))
    vec = lambda v: v[:, None, :]
    gq, gkv, p_scale = vec(mla_gq), vec(mla_gkv), vec(pool_scale)
    c_b, c_g, c_beta = vec(conv_dw_b), vec(conv_ln_g), vec(conv_ln_b)
    g1, b1, g2, b2 = vec(ln1_g), vec(ln1_b), vec(ln2_g), vec(ln2_b)
    f_b = vec(ffn_dw_b)

    xf = x.reshape(m, d_model)
    xb = xf.astype(BF16)
    for l in range(depth):
        q, k, v = _latent(xb, w_lat, gq, gkv, w_uq, w_ukv, l, mla_tabs, tm=512)
        h_a = _mla_attention(q, k, v, batch, seq)

        u_pool = _in_proj(xb, w_in_r, l, OFF_POOL, POOL_WIDTH, F32, 1024, 1024, "pool_in")
        h_b = _pool_mixer(u_pool, pool_w_bf, p_scale, l, batch, seq)

        u_conv = _in_proj(xb, w_in_r, l, OFF_CONV, 2 * CONV_WIDTH, F32, 1024, 1024, "conv_in")
        h_c = _conv_mixer(u_conv, conv_dw_p, c_b, c_g, c_beta, l, batch, seq)

        qkv = _dil_proj(xb, w_in_r, l, dil_tabs, tm=1024)
        h_d = _dil_attention(qkv, batch, seq)

        merged = _merge(xb, (h_a, h_b, h_c, h_d), w_in_r, w_projs, b_gate, l)
        xf, xb = _proj_ln(merged, w_out, xf, g1, b1, l, alpha, tm=512, tk=d_model, name="mix_out_ln1")

        hid = _ffn_up(xb, ffn_w_up, ffn_dw, f_b, l, seq)
        xf, xb = _proj_ln(hid, w_down, xf, g2, b2, l, alpha, tm=512, tk=ffn_dim // 4, name="ffn_down_ln2")
    return xf.reshape(batch, seq, d_model)
```

```python
import functools

import jax
import jax.numpy as jnp
from jax import lax
from jax.experimental import pallas as pl
from jax.experimental.pallas import tpu as pltpu

MLA_HEADS = 8
MLA_LORA = 512
MLA_NOPE = 128
MLA_ROPE = 64
MLA_V = 128
MLA_QK = MLA_NOPE + MLA_ROPE
POOL_WINDOWS = (2, 4, 8, 16)
POOL_GROUP = 256
POOL_WIDTH = 1024
CONV_WIDTH = 1024
CONV_K = 31
DIL_CONFIGS = ((128, 1), (512, 4), (2048, 16))
DIL_HEADS = 8
DIL_DIM = 128
DIL_WIDTH = DIL_HEADS * DIL_DIM
DIL_ROT = DIL_DIM // 4
DIL_BLOCK = 128
N_BRANCHES = 4
ROPE_THETA = 500000.0
FFN_K = 3
LN_EPS = 1e-5
RMS_EPS = 1e-6
NEG_INF = -1e30
LOG2_E = 1.4426950408889634

OFF_POOL = 0
OFF_CONV = OFF_POOL + POOL_WIDTH
OFF_DIL = OFF_CONV + 2 * CONV_WIDTH
OFF_GATE = OFF_DIL + len(DIL_CONFIGS) * 3 * DIL_WIDTH

LANES = 128
V7X_VMEM_BYTES = 64 * 1024 * 1024
VMEM_CAP = V7X_VMEM_BYTES - 8 * 1024 * 1024

BF16 = jnp.bfloat16
F32 = jnp.float32


def _params(semantics, block_bytes):
    limit = min(VMEM_CAP, max(32 * 1024 * 1024, 2 * block_bytes + 16 * 1024 * 1024))
    return pltpu.CompilerParams(dimension_semantics=semantics, vmem_limit_bytes=limit)


def _nbytes(shape, dtype):
    n = 1
    for s in shape:
        n *= s
    return n * jnp.dtype(dtype).itemsize


def _dot(a, b):
    return jnp.dot(a, b, preferred_element_type=F32)


def _dot_nt(a, b):
    return lax.dot_general(a, b, (((1,), (1,)), ((), ())), preferred_element_type=F32)


def _layer_spec(block, layer, index_map):
    return pl.BlockSpec((None,) + tuple(block), lambda *g: (layer,) + tuple(index_map(*g)))


def _rope_lanes(x, cos, sin_lo, sin_hi, half):
    return (x * cos + pltpu.roll(x, LANES - half, 1) * sin_lo
            + pltpu.roll(x, half, 1) * sin_hi)


def _rope_table_kernel(pos_ref, c_ref, cos_a, lo_a, hi_a, cos_b, lo_b, hi_b):
    pos = pos_ref[...]
    for base, (co, lo, hi) in ((0, (cos_a, lo_a, hi_a)), (3, (cos_b, lo_b, hi_b))):
        ang = pos * c_ref[base:base + 1, :]
        sn = jnp.sin(ang)
        co[...] = jnp.cos(ang)
        lo[...] = sn * c_ref[base + 1:base + 2, :]
        hi[...] = sn * c_ref[base + 2:base + 3, :]


def _lane_pattern(rot_dim):
    half = rot_dim // 2
    inv_freq = ROPE_THETA ** (-jnp.arange(half, dtype=F32) * 2.0 / rot_dim)
    lane = jnp.arange(LANES)
    freq = jnp.where(lane < rot_dim, jnp.tile(inv_freq, LANES // half), 0.0)
    lo = jnp.where(lane < half, -1.0, 0.0)
    hi = jnp.where((lane >= half) & (lane < rot_dim), 1.0, 0.0)
    return [freq.astype(F32), lo.astype(F32), hi.astype(F32)]


def _rope_tables(positions):
    m = positions.size
    pos = jnp.broadcast_to(positions.reshape(m, 1).astype(F32), (m, LANES))
    consts = jnp.stack(_lane_pattern(MLA_ROPE) + _lane_pattern(DIL_ROT)
                       + [jnp.zeros((LANES,), F32)] * 2)
    tm = 1024
    spec = pl.BlockSpec((tm, LANES), lambda i: (i, 0))
    out = jax.ShapeDtypeStruct((m, LANES), F32)
    return pl.pallas_call(
        _rope_table_kernel, grid=(m // tm,),
        in_specs=[spec, pl.BlockSpec((8, LANES), lambda i: (0, 0))],
        out_specs=[spec] * 6, out_shape=[out] * 6, name="rope_tables",
        compiler_params=_params(("parallel",), 7 * _nbytes((tm, LANES), F32)),
    )(pos, consts)


def _mm_kernel(a_ref, w_ref, o_ref):
    o_ref[...] = _dot(a_ref[...], w_ref[...]).astype(o_ref.dtype)


def _in_proj(a, w_in, layer, col_off, n, out_dtype, tm, tn, name):
    m, k = a.shape
    c0 = col_off // tn
    blk = (_nbytes((tm, k), a.dtype) + _nbytes((k, tn), w_in.dtype) + _nbytes((tm, tn), out_dtype))
    return pl.pallas_call(
        _mm_kernel, grid=(n // tn, m // tm),
        in_specs=[pl.BlockSpec((tm, k), lambda j, i: (i, 0)),
                  _layer_spec((k, tn), layer, lambda j, i: (0, c0 + j))],
        out_specs=pl.BlockSpec((tm, tn), lambda j, i: (i, j)),
        out_shape=jax.ShapeDtypeStruct((m, n), out_dtype), name=name,
        compiler_params=_params(("parallel", "parallel"), blk),
    )(a, w_in)


MXU_COLS = 256


def _skewed(i, nt, bufs, compute_chunks, epilogue_chunks):
    def run(cur, prv, do_compute, do_epilogue):
        cs = compute_chunks(cur, prv) if do_compute else []
        es = epilogue_chunks(prv) if do_epilogue else []
        per = -(-len(es) // len(cs)) if cs else len(es)
        for k in range(max(len(cs), 1)):
            if cs:
                cs[k]()
            for e in es[k * per:(k + 1) * per]:
                e()

    parity = lax.rem(i, 2)
    middle = (i > 0) & (i < nt)
    pl.when(i == 0)(lambda: run(bufs[0], None, True, False))
    pl.when(middle & (parity == 0))(lambda: run(bufs[0], bufs[1], True, True))
    pl.when(middle & (parity == 1))(lambda: run(bufs[1], bufs[0], True, True))
    pl.when(i == nt)(lambda: run(None, bufs[1 - nt % 2], False, True))


def _skewed_in(tm_width, nt):
    return pl.BlockSpec(tm_width, lambda j, i: (jnp.minimum(i, nt - 1), 0))


def _dil_proj_kernel(a_ref, w_ref, cos_ref, lo_ref, hi_ref, o_ref, acc_a, acc_b, *, nt):
    tm = o_ref.shape[0]
    is_v = lax.rem(pl.program_id(0), 3) == 2

    def compute_chunks(cur, prv):
        def chunk(c):
            cs = slice(c * MXU_COLS, (c + 1) * MXU_COLS)
            cur[:, cs] = _dot(a_ref[...], w_ref[:, cs])
        return [functools.partial(chunk, c) for c in range(DIL_WIDTH // MXU_COLS)]

    def epilogue_chunks(prv):
        half = tm // 2

        def chunk(h, r):
            rows = slice(r * half, (r + 1) * half)
            sl = slice(h * DIL_DIM, (h + 1) * DIL_DIM)
            cos = jnp.where(is_v, 1.0, cos_ref[rows, :])
            lo = jnp.where(is_v, 0.0, lo_ref[rows, :])
            hi = jnp.where(is_v, 0.0, hi_ref[rows, :])
            o_ref[rows, sl] = _rope_lanes(prv[rows, sl], cos, lo, hi, DIL_ROT // 2).astype(o_ref.dtype)
        return [functools.partial(chunk, h, r) for h in range(DIL_HEADS) for r in range(2)]

    _skewed(pl.program_id(1), nt, (acc_a, acc_b), compute_chunks, epilogue_chunks)


def _dil_proj(x_bf, w_in, layer, tabs, tm):
    m, k = x_bf.shape
    n = len(DIL_CONFIGS) * 3 * DIL_WIDTH
    tn = DIL_WIDTH
    nt = m // tm
    c0 = OFF_DIL // tn
    prev_row = lambda j, i: (jnp.maximum(i - 1, 0), 0)
    tab_spec = pl.BlockSpec((tm, LANES), prev_row)
    blk = (_nbytes((tm, k), BF16) + _nbytes((k, tn), BF16) + _nbytes((tm, tn), BF16)
           + 3 * _nbytes((tm, LANES), F32) + _nbytes((tm, tn), F32))
    return pl.pallas_call(
        functools.partial(_dil_proj_kernel, nt=nt), grid=(n // tn, nt + 1),
        in_specs=[_skewed_in((tm, k), nt),
                  _layer_spec((k, tn), layer, lambda j, i: (0, c0 + j)),
                  tab_spec, tab_spec, tab_spec],
        out_specs=pl.BlockSpec((tm, tn), lambda j, i: (jnp.maximum(i - 1, 0), j)),
        out_shape=jax.ShapeDtypeStruct((m, n), BF16),
        scratch_shapes=[pltpu.VMEM((tm, tn), F32)] * 2, name="dil_proj",
        compiler_params=_params(("parallel", "arbitrary"), blk),
    )(x_bf, w_in, *tabs)


def _latent_kernel(x_ref, wlat_ref, gq_ref, gkv_ref, wuq_ref, wukv_ref,
                   cos_ref, lo_ref, hi_ref, q_ref, k_ref, v_ref):
    lat = _dot(x_ref[...], wlat_ref[...])
    cos, lo, hi = cos_ref[...], lo_ref[...], hi_ref[...]

    def rms(c, g):
        return c * lax.rsqrt(jnp.mean(jnp.square(c), axis=-1, keepdims=True) + RMS_EPS) * g

    def rope(v):
        return _rope_lanes(v, cos, lo, hi, MLA_ROPE // 2)

    cq = rms(lat[:, :MLA_LORA], gq_ref[...]).astype(BF16)
    ckv = rms(lat[:, MLA_LORA:2 * MLA_LORA], gkv_ref[...]).astype(BF16)
    kpe = rope(lat[:, 2 * MLA_LORA:]).astype(BF16)
    q = _dot(cq, wuq_ref[...])
    kv = _dot(ckv, wukv_ref[...])
    for h in range(MLA_HEADS):
        b0 = h * 2 * LANES
        q_ref[:, b0:b0 + LANES] = q[:, b0:b0 + LANES].astype(BF16)
        q_ref[:, b0 + LANES:b0 + 2 * LANES] = rope(q[:, b0 + LANES:b0 + 2 * LANES]).astype(BF16)
        k_ref[:, b0:b0 + LANES] = kv[:, h * LANES:(h + 1) * LANES].astype(BF16)
        k_ref[:, b0 + LANES:b0 + 2 * LANES] = kpe
    v_ref[...] = kv[:, MLA_HEADS * MLA_NOPE:].astype(BF16)


def _latent(x_bf, w_lat, gq, gkv, w_uq, w_ukv, layer, tabs, tm):
    m, k = x_bf.shape
    nl = w_lat.shape[2]
    hq = MLA_HEADS * 2 * LANES
    hv = MLA_HEADS * MLA_V
    zero2 = lambda i: (0, 0)
    row = lambda w: pl.BlockSpec((tm, w), lambda i: (i, 0))
    blk = (_nbytes((tm, k), BF16) + _nbytes((k, nl), BF16) + 2 * _nbytes((MLA_LORA, hq), BF16)
           + 2 * _nbytes((tm, hq), BF16) + _nbytes((tm, hv), BF16) + 3 * _nbytes((tm, LANES), F32))
    return pl.pallas_call(
        _latent_kernel, grid=(m // tm,),
        in_specs=[row(k), _layer_spec((k, nl), layer, zero2),
                  _layer_spec((1, MLA_LORA), layer, zero2), _layer_spec((1, MLA_LORA), layer, zero2),
                  _layer_spec((MLA_LORA, hq), layer, zero2), _layer_spec((MLA_LORA, hq), layer, zero2),
                  row(LANES), row(LANES), row(LANES)],
        out_specs=[row(hq), row(hq), row(hv)],
        out_shape=[jax.ShapeDtypeStruct((m, hq), BF16), jax.ShapeDtypeStruct((m, hq), BF16),
                   jax.ShapeDtypeStruct((m, hv), BF16)],
        name="mla_latent",
        compiler_params=_params(("parallel",), blk),
    )(x_bf, w_lat, gq, gkv, w_uq, w_ukv, *tabs)


def _mla_attn_kernel(q_ref, k_ref, v_ref, o_ref, *, tq, scale):
    s = q_ref.shape[0]
    c = scale * LOG2_E
    diag = (lax.broadcasted_iota(jnp.int32, (tq, tq), 1)
            <= lax.broadcasted_iota(jnp.int32, (tq, tq), 0))
    for qi in range(s // tq):
        s0, s1 = qi * tq, (qi + 1) * tq
        q = q_ref[s0:s1, :]
        sd = jnp.where(diag, _dot_nt(q, k_ref[s0:s1, :]), NEG_INF)
        mx = jnp.max(sd, axis=-1, keepdims=True)
        if qi > 0:
            so = _dot_nt(q, k_ref[:s0, :])
            mx = jnp.maximum(mx, jnp.max(so, axis=-1, keepdims=True))
            po = jnp.exp2((so - mx) * c)
        pd = jnp.exp2((sd - mx) * c)
        den = jnp.sum(pd, axis=-1, keepdims=True)
        o = _dot(pd.astype(BF16), v_ref[s0:s1, :])
        if qi > 0:
            den = den + jnp.sum(po, axis=-1, keepdims=True)
            o = o + _dot(po.astype(BF16), v_ref[:s0, :])
        o_ref[s0:s1, :] = (o / den).astype(o_ref.dtype)


def _mla_attention(q, k, v, batch, seq, tq=256):
    m = q.shape[0]
    dq = 2 * LANES
    blk = 2 * _nbytes((seq, dq), BF16) + 2 * _nbytes((seq, MLA_V), BF16)
    return pl.pallas_call(
        functools.partial(_mla_attn_kernel, tq=tq, scale=MLA_QK ** -0.5),
        grid=(batch, MLA_HEADS),
        in_specs=[pl.BlockSpec((seq, dq), lambda b, h: (b, h)),
                  pl.BlockSpec((seq, dq), lambda b, h: (b, h)),
                  pl.BlockSpec((seq, MLA_V), lambda b, h: (b, h))],
        out_specs=pl.BlockSpec((seq, MLA_V), lambda b, h: (b, h)),
        out_shape=jax.ShapeDtypeStruct((m, MLA_HEADS * MLA_V), BF16), name="mla_attention",
        compiler_params=_params(("parallel", "parallel"), blk + 4 * _nbytes((tq, seq), F32)),
    )(q, k, v)


DIL_UNROLL = 16


def _dil_attn_kernel(q1, k1, v1, q2, k2, v2, q3, k3, v3, o_ref, stage, o_acc, lse_acc, *, seq, scale):
    nb = DIL_BLOCK
    diff = (lax.broadcasted_iota(jnp.int32, (nb, 2 * nb), 0)
            - lax.broadcasted_iota(jnp.int32, (nb, 2 * nb), 1))
    band = (diff + nb >= 0) & (diff <= 0)
    causal = (lax.broadcasted_iota(jnp.int32, (nb, nb), 1)
              <= lax.broadcasted_iota(jnp.int32, (nb, nb), 0))

    def attend_many(gi, blocks):
        scs = [jnp.where(mask, _dot_nt(q, k), NEG_INF) for _, q, k, _, mask in blocks]
        mxs = [jnp.max(sc, axis=-1, keepdims=True) for sc in scs]
        es = [jnp.exp2((sc - mx) * (scale * LOG2_E)) for sc, mx in zip(scs, mxs)]
        dens = [jnp.sum(e, axis=-1, keepdims=True) for e in es]
        ps = [(e / den).astype(BF16) for e, den in zip(es, dens)]
        outs = [_dot(p, blk[3]) for p, blk in zip(ps, blocks)]
        for blk, o, mx, den in zip(blocks, outs, mxs, dens):
            o_acc[gi, blk[0], :] = o
            lse_acc[gi, blk[0], :] = jnp.broadcast_to(mx * scale + jnp.log(den), (nb, LANES))

    def block_body(it, carry):
        blocks = []
        for u in range(DIL_UNROLL):
            blk = it * DIL_UNROLL + u
            q0 = pl.multiple_of(blk * nb, nb)
            if u == 0:
                k0 = pl.multiple_of(jnp.maximum(blk - 1, 0) * nb, nb)
                rel = diff + (q0 - k0)
                mask = (rel >= 0) & (rel <= nb)
            else:
                k0 = pl.multiple_of(blk * nb - nb, nb)
                mask = band
            blocks.append((pl.ds(q0, nb), q1[pl.ds(q0, nb), :], k1[pl.ds(k0, 2 * nb), :],
                           v1[pl.ds(k0, 2 * nb), :], mask))
        attend_many(0, blocks)
        return carry

    lax.fori_loop(0, seq // (nb * DIL_UNROLL), block_body, 0)

    for gi, (q_ref, k_ref, v_ref) in ((1, (q2, k2, v2)), (2, (q3, k3, v3))):
        dil = DIL_CONFIGS[gi][1]
        length = seq // dil
        nblk = length // nb
        per_iter = DIL_UNROLL // nblk
        stage[0] = q_ref[...].astype(F32)
        stage[1] = k_ref[...].astype(F32)
        stage[2] = v_ref[...].astype(F32)

        for it in range(dil // per_iter):
            blocks = []
            for r in range(it * per_iter, (it + 1) * per_iter):
                qc, kc, vc = [stage[j, pl.ds(r, length, stride=dil), :].astype(BF16) for j in range(3)]
                for blk in range(nblk):
                    rows = pl.ds(r + blk * nb * dil, nb, stride=dil)
                    q_blk = qc[blk * nb:(blk + 1) * nb]
                    if blk == 0:
                        blocks.append((rows, q_blk, kc[0:nb], vc[0:nb], causal))
                    else:
                        blocks.append((rows, q_blk, kc[(blk - 1) * nb:(blk + 1) * nb],
                                       vc[(blk - 1) * nb:(blk + 1) * nb], band))
            attend_many(gi, blocks)

    chunk = 2 * nb

    def merge_body(c, carry):
        rows = pl.ds(pl.multiple_of(c * chunk, chunk), chunk)
        la, lb, lc = lse_acc[0, rows, :], lse_acc[1, rows, :], lse_acc[2, rows, :]
        mx = jnp.maximum(jnp.maximum(la, lb), lc)
        ea, eb, ec = jnp.exp(la - mx), jnp.exp(lb - mx), jnp.exp(lc - mx)
        den = ea + eb + ec
        out = ((ea / den) * o_acc[0, rows, :] + (eb / den) * o_acc[1, rows, :]
               + (ec / den) * o_acc[2, rows, :])
        o_ref[rows, :] = out.astype(o_ref.dtype)
        return carry

    lax.fori_loop(0, seq // chunk, merge_body, 0)


def _dil_attention(qkv, batch, seq):
    m = qkv.shape[0]
    col_blocks = DIL_WIDTH // LANES
    specs = [pl.BlockSpec((seq, LANES), lambda b, h, c=(g * 3 + j) * col_blocks: (b, c + h))
             for g in range(len(DIL_CONFIGS)) for j in range(3)]
    blk = 10 * _nbytes((seq, LANES), BF16)
    scratch = 9 * _nbytes((seq, LANES), F32)
    return pl.pallas_call(
        functools.partial(_dil_attn_kernel, seq=seq, scale=DIL_DIM ** -0.5),
        grid=(batch, DIL_HEADS),
        in_specs=specs,
        out_specs=pl.BlockSpec((seq, LANES), lambda b, h: (b, h)),
        out_shape=jax.ShapeDtypeStruct((m, DIL_WIDTH), BF16),
        scratch_shapes=[pltpu.VMEM((3, seq, LANES), F32)] * 3,
        name="dil_attention",
        compiler_params=_params(("parallel", "parallel"), blk + scratch // 2),
    )(*([qkv] * 9))


POOL_HALO = 16


def _pool_kernel(cur_ref, halo_ref, w_ref, scale_ref, o_ref, buf, *, ts):
    i = pl.program_id(1)
    buf[0:POOL_HALO, :] = jnp.where(i > 0, halo_ref[...], 0.0)
    buf[POOL_HALO:, :] = cur_ref[...]
    t = i * ts + lax.broadcasted_iota(jnp.int32, (ts, 1), 0)
    for g, win in enumerate(POOL_WINDOWS):
        sl = slice(g * POOL_GROUP, (g + 1) * POOL_GROUP)
        tok = buf[POOL_HALO:POOL_HALO + ts, sl]
        acc = tok
        for j in range(1, win):
            acc = acc + buf[POOL_HALO - j:POOL_HALO - j + ts, sl]
        cnt = jnp.minimum(t + 1, win).astype(F32)
        d = (acc / cnt - tok).astype(BF16)
        o_ref[:, sl] = (_dot(d, w_ref[g]) * scale_ref[:, sl]).astype(o_ref.dtype)


def _pool_mixer(u_pool, w_grp, scale, layer, batch, seq, ts=256):
    m, c = u_pool.shape
    nt = seq // ts
    hb = ts // POOL_HALO
    blk = (_nbytes((ts, c), F32) + _nbytes((POOL_HALO, c), F32) + _nbytes(w_grp.shape[1:], BF16)
           + _nbytes((ts, c), BF16) + _nbytes((ts + POOL_HALO, c), F32))
    return pl.pallas_call(
        functools.partial(_pool_kernel, ts=ts), grid=(batch, nt),
        in_specs=[pl.BlockSpec((ts, c), lambda b, i: (b * nt + i, 0)),
                  pl.BlockSpec((POOL_HALO, c), lambda b, i: ((b * nt + i) * hb - jnp.minimum(i, 1), 0)),
                  _layer_spec(w_grp.shape[1:], layer, lambda b, i: (0, 0, 0)),
                  _layer_spec((1, c), layer, lambda b, i: (0, 0))],
        out_specs=pl.BlockSpec((ts, c), lambda b, i: (b * nt + i, 0)),
        out_shape=jax.ShapeDtypeStruct((m, c), BF16),
        scratch_shapes=[pltpu.VMEM((ts + POOL_HALO, c), F32)], name="pool_mixer",
        compiler_params=_params(("parallel", "parallel"), blk),
    )(u_pool, u_pool, w_grp, scale)


CONV_HALO = 32


SUBLANES = 8
CONV_ROWS = 64


def _conv_kernel(cur_ref, halo_ref, w_ref, b_ref, g_ref, beta_ref, o_ref, buf, shifted, hbuf, *, ts):
    i = pl.program_id(1)
    c = CONV_WIDTH

    def glu(u):
        return u[:, :c] * jax.nn.sigmoid(u[:, c:])

    buf[0:CONV_HALO, :] = jnp.where(i > 0, glu(halo_ref[...]), 0.0)

    def glu_body(j, carry):
        r0 = pl.multiple_of(j * CONV_HALO, CONV_HALO)
        buf[pl.ds(CONV_HALO + r0, CONV_HALO), :] = glu(cur_ref[pl.ds(r0, CONV_HALO), :])
        return carry

    lax.fori_loop(0, ts // CONV_HALO, glu_body, 0)
    span = ts + CONV_HALO - SUBLANES
    for r in range(1, SUBLANES):
        shifted[r - 1, 0:span, :] = buf[r:r + span, :]
    off = CONV_HALO - (CONV_K - 1)
    for cc in range(c // LANES):
        sl = slice(cc * LANES, (cc + 1) * LANES)
        for r0 in range(0, ts, CONV_ROWS):
            acc = jnp.zeros((CONV_ROWS, LANES), F32)
            for k in range(CONV_K):
                r = (off + k) % SUBLANES
                base = off + k - r + r0
                win = (buf[base:base + CONV_ROWS, sl] if r == 0
                       else shifted[r - 1, base:base + CONV_ROWS, sl])
                acc = acc + w_ref[k:k + 1, sl] * win
            hbuf[r0:r0 + CONV_ROWS, sl] = acc + b_ref[:, sl]
    ln_rows = 4 * CONV_HALO

    def ln_body(j, carry):
        rows = pl.ds(pl.multiple_of(j * ln_rows, ln_rows), ln_rows)
        h = hbuf[rows, :]
        mu = jnp.mean(h, axis=-1, keepdims=True)
        var = jnp.mean(jnp.square(h - mu), axis=-1, keepdims=True)
        y = (h - mu) * lax.rsqrt(var + LN_EPS) * g_ref[...] + beta_ref[...]
        o_ref[rows, :] = (y * jax.nn.sigmoid(y)).astype(o_ref.dtype)
        return carry

    lax.fori_loop(0, ts // ln_rows, ln_body, 0)


def _conv_mixer(u_conv, w_dw, b_dw, ln_g, ln_b, layer, batch, seq, ts=256):
    m, c2 = u_conv.shape
    c = c2 // 2
    nt = seq // ts
    hb = ts // CONV_HALO
    vec = _layer_spec((1, c), layer, lambda b, i: (0, 0))
    blk = (_nbytes((ts, c2), F32) + _nbytes((CONV_HALO, c2), F32) + _nbytes(w_dw.shape[1:], F32)
           + _nbytes((ts, c), BF16) + 2 * _nbytes((ts + CONV_HALO, c), F32))
    return pl.pallas_call(
        functools.partial(_conv_kernel, ts=ts), grid=(batch, nt),
        in_specs=[pl.BlockSpec((ts, c2), lambda b, i: (b * nt + i, 0)),
                  pl.BlockSpec((CONV_HALO, c2), lambda b, i: ((b * nt + i) * hb - jnp.minimum(i, 1), 0)),
                  _layer_spec(w_dw.shape[1:], layer, lambda b, i: (0, 0)), vec, vec, vec],
        out_specs=pl.BlockSpec((ts, c), lambda b, i: (b * nt + i, 0)),
        out_shape=jax.ShapeDtypeStruct((m, c), BF16),
        scratch_shapes=[pltpu.VMEM((ts + CONV_HALO, c), F32),
                        pltpu.VMEM((SUBLANES - 1, ts + CONV_HALO, c), F32), pltpu.VMEM((ts, c), F32)],
        name="conv_mixer",
        compiler_params=_params(("parallel", "parallel"), blk + 4 * _nbytes((ts + CONV_HALO, c), F32)),
    )(u_conv, u_conv, w_dw, b_dw, ln_g, ln_b)


def _merge_kernel(x_ref, ha, hb, hc, hd, g0, g1, g2, g3, pa, pb, pc, pd, bias_ref, o_ref):
    x = x_ref[...]
    acc = None
    for b, (h_ref, wg_ref, wp_ref) in enumerate(((ha, g0, pa), (hb, g1, pb), (hc, g2, pc), (hd, g3, pd))):
        gate = jax.nn.sigmoid(_dot(x, wg_ref[...]) + bias_ref[b:b + 1, :])
        term = gate * _dot(h_ref[...], wp_ref[...])
        acc = term if acc is None else acc + term
    o_ref[...] = acc.astype(o_ref.dtype)


def _merge(x_bf, hs, w_in, w_projs, b_gate, layer, tm=512, tn=512):
    m, d = x_bf.shape
    kb = hs[0].shape[1]
    nj = d // tn
    c0 = OFF_GATE // tn
    gate_spec = lambda b: _layer_spec((d, tn), layer, lambda j, i: (0, c0 + b * nj + j))
    blk = (_nbytes((tm, d), BF16) + 4 * _nbytes((tm, kb), BF16) + 4 * _nbytes((d, tn), BF16)
           + 4 * _nbytes((kb, tn), BF16) + _nbytes((tm, tn), BF16) + 4 * _nbytes((tm, tn), F32))
    return pl.pallas_call(
        _merge_kernel, grid=(nj, m // tm),
        in_specs=([pl.BlockSpec((tm, d), lambda j, i: (i, 0))]
                  + [pl.BlockSpec((tm, kb), lambda j, i: (i, 0))] * 4
                  + [gate_spec(b) for b in range(N_BRANCHES)]
                  + [_layer_spec((kb, tn), layer, lambda j, i: (0, j))] * 4
                  + [_layer_spec((N_BRANCHES, tn), layer, lambda j, i: (0, j))]),
        out_specs=pl.BlockSpec((tm, tn), lambda j, i: (i, j)),
        out_shape=jax.ShapeDtypeStruct((m, d), BF16), name="gated_merge",
        compiler_params=_params(("parallel", "parallel"), blk),
    )(x_bf, *hs, w_in, w_in, w_in, w_in, *w_projs, b_gate)


def _proj_ln_kernel(a_ref, w_ref, res_ref, g_ref, b_ref, of_ref, ob_ref, acc_ref, *, alpha):
    k = pl.program_id(1)

    @pl.when(k == 0)
    def _():
        acc_ref[...] = jnp.zeros_like(acc_ref)

    acc_ref[...] += _dot(a_ref[...], w_ref[...])

    @pl.when(k == pl.num_programs(1) - 1)
    def _():
        y = alpha * res_ref[...] + acc_ref[...]
        mu = jnp.mean(y, axis=-1, keepdims=True)
        var = jnp.mean(jnp.square(y - mu), axis=-1, keepdims=True)
        out = (y - mu) * lax.rsqrt(var + LN_EPS) * g_ref[...] + b_ref[...]
        of_ref[...] = out
        ob_ref[...] = out.astype(BF16)


def _proj_ln(a, w, res, g, b, layer, alpha, tm, tk, name):
    m, k = a.shape
    d = w.shape[2]
    vec = _layer_spec((1, d), layer, lambda i, kk: (0, 0))
    row = pl.BlockSpec((tm, d), lambda i, kk: (i, 0))
    blk = (_nbytes((tm, tk), BF16) + _nbytes((tk, d), BF16) + 2 * _nbytes((tm, d), F32)
           + _nbytes((tm, d), BF16) + _nbytes((tm, d), F32))
    return pl.pallas_call(
        functools.partial(_proj_ln_kernel, alpha=alpha), grid=(m // tm, k // tk),
        in_specs=[pl.BlockSpec((tm, tk), lambda i, kk: (i, kk)),
                  _layer_spec((tk, d), layer, lambda i, kk: (kk, 0)), row, vec, vec],
        out_specs=[row, row],
        out_shape=[jax.ShapeDtypeStruct((m, d), F32), jax.ShapeDtypeStruct((m, d), BF16)],
        scratch_shapes=[pltpu.VMEM((tm, d), F32)], name=name,
        compiler_params=_params(("parallel", "arbitrary"), blk),
    )(a, w, res, g, b)


FFN_HALO = 8


def _ffn_up_kernel(x_ref, wv_ref, wg_ref, dwv_ref, dwg_ref, bv_ref, bg_ref, o_ref,
                   wv_bf, wg_bf, bufv, bufg, *, tm, tiles_per_seq):
    i = pl.program_id(1)
    first = lax.rem(i, tiles_per_seq) == 0

    @pl.when(i == 0)
    def _():
        wv_bf[...] = wv_ref[...].astype(BF16)
        wg_bf[...] = wg_ref[...].astype(BF16)

    @pl.when(first)
    def _():
        bufv[0:FFN_HALO, :] = jnp.zeros((FFN_HALO, bufv.shape[1]), F32)
        bufg[0:FFN_HALO, :] = jnp.zeros((FFN_HALO, bufg.shape[1]), F32)

    @pl.when(jnp.logical_not(first))
    def _():
        bufv[0:FFN_HALO, :] = bufv[tm:tm + FFN_HALO, :]
        bufg[0:FFN_HALO, :] = bufg[tm:tm + FFN_HALO, :]

    def conv(buf, w_ref, b_ref, cs, r0, n):
        full = buf[r0:r0 + n + FFN_HALO, cs]
        acc = None
        for k in range(FFN_K):
            back = FFN_K - 1 - k
            win = (pltpu.roll(full, back, 0) if back else full)[FFN_HALO:]
            term = w_ref[k:k + 1, cs] * win
            acc = term if acc is None else acc + term
        return acc + b_ref[:, cs]

    x = x_ref[...]
    for c in range(o_ref.shape[1] // MXU_COLS):
        cs = slice(c * MXU_COLS, (c + 1) * MXU_COLS)
        bufv[FFN_HALO:, cs] = _dot(x, wv_bf[:, cs])
        bufg[FFN_HALO:, cs] = _dot(x, wg_bf[:, cs])
        val = conv(bufv, dwv_ref, bv_ref, cs, 0, tm)
        gate = conv(bufg, dwg_ref, bg_ref, cs, 0, tm)
        o_ref[:, cs] = (val * (gate * jax.nn.sigmoid(gate))).astype(o_ref.dtype)


def _ffn_up(x_bf, w_up, dw, dw_b, layer, seq, tm=1024, tn=512):
    m, d = x_bf.shape
    f = w_up.shape[2] // 2
    nj = f // tn
    blk = (_nbytes((tm, d), BF16) + 2 * _nbytes((d, tn), F32) + _nbytes((d, tn), BF16)
           + _nbytes((tm, tn), BF16) + _nbytes((tm + FFN_HALO, tn), F32))
    return pl.pallas_call(
        functools.partial(_ffn_up_kernel, tm=tm, tiles_per_seq=seq // tm), grid=(nj, m // tm),
        in_specs=[pl.BlockSpec((tm, d), lambda j, i: (i, 0)),
                  _layer_spec((d, tn), layer, lambda j, i: (0, j)),
                  _layer_spec((d, tn), layer, lambda j, i: (0, nj + j)),
                  _layer_spec((FFN_K, tn), layer, lambda j, i: (0, j)),
                  _layer_spec((FFN_K, tn), layer, lambda j, i: (0, nj + j)),
                  _layer_spec((1, tn), layer, lambda j, i: (0, j)),
                  _layer_spec((1, tn), layer, lambda j, i: (0, nj + j))],
        out_specs=pl.BlockSpec((tm, tn), lambda j, i: (i, j)),
        out_shape=jax.ShapeDtypeStruct((m, f), BF16),
        scratch_shapes=[pltpu.VMEM((d, tn), BF16)] * 2 + [pltpu.VMEM((tm + FFN_HALO, tn), F32)] * 2,
        name="ffn_up",
        compiler_params=_params(("parallel", "arbitrary"), blk),
    )(x_bf, w_up, w_up, dw, dw, dw_b, dw_b)


def _prep_in_proj(w_in):
    pe1 = 2 * MLA_LORA + MLA_ROPE
    w_lat = jnp.pad(w_in[:, :, :pe1], ((0, 0), (0, 0), (0, LANES - MLA_ROPE))).astype(BF16)
    return w_lat, w_in.astype(BF16)[:, :, pe1:]


def _prep_mla_up(mla_w_uq, mla_w_ukv):
    depth = mla_w_uq.shape[0]
    w_uq = jnp.pad(mla_w_uq.reshape(depth, MLA_LORA, MLA_HEADS, MLA_QK),
                   ((0, 0), (0, 0), (0, 0), (0, 2 * LANES - MLA_QK)))
    w_uq = w_uq.reshape(depth, MLA_LORA, -1).astype(BF16)
    ukv = mla_w_ukv.reshape(depth, MLA_LORA, MLA_HEADS, MLA_NOPE + MLA_V)
    w_ukv = jnp.concatenate([ukv[..., :MLA_NOPE].reshape(depth, MLA_LORA, -1),
                             ukv[..., MLA_NOPE:].reshape(depth, MLA_LORA, -1)], axis=2).astype(BF16)
    return w_uq, w_ukv


def kernel(x, positions, w_in, b_gate, mla_gq, mla_gkv, mla_w_uq, mla_w_ukv, mla_w_proj, pool_w, pool_scale, pool_w_proj, conv_dw, conv_dw_b, conv_ln_g, conv_ln_b, conv_w_proj, dil_w_proj, mix_w_out, ln1_g, ln1_b, ffn_w_up, ffn_dw, ffn_dw_b, ffn_w_down, ln2_g, ln2_b):
    batch, seq, d_model = x.shape
    depth = w_in.shape[0]
    m = batch * seq
    alpha = (2 * depth) ** 0.25
    ffn_dim = ffn_w_down.shape[1]

    tabs = _rope_tables(positions)
    mla_tabs, dil_tabs = tabs[:3], tabs[3:]

    w_lat, w_in_r = _prep_in_proj(w_in)
    w_uq, w_ukv = _prep_mla_up(mla_w_uq, mla_w_ukv)
    w_projs = tuple(w.astype(BF16) for w in (mla_w_proj, pool_w_proj, conv_w_proj, dil_w_proj))
    w_out = mix_w_out.astype(BF16)
    w_down = ffn_w_down.astype(BF16)
    pool_w_bf = pool_w.astype(BF16)
    conv_dw_p = jnp.pad(conv_dw, ((0, 0), (0, 1), (0, 0)))
    vec = lambda v: v[:, None, :]
    gq, gkv, p_scale = vec(mla_gq), vec(mla_gkv), vec(pool_scale)
    c_b, c_g, c_beta = vec(conv_dw_b), vec(conv_ln_g), vec(conv_ln_b)
    g1, b1, g2, b2 = vec(ln1_g), vec(ln1_b), vec(ln2_g), vec(ln2_b)
    f_b = vec(ffn_dw_b)

    xf = x.reshape(m, d_model)
    xb = xf.astype(BF16)
    for l in range(depth):
        q, k, v = _latent(xb, w_lat, gq, gkv, w_uq, w_ukv, l, mla_tabs, tm=512)
        h_a = _mla_attention(q, k, v, batch, seq)

        u_pool = _in_proj(xb, w_in_r, l, OFF_POOL, POOL_WIDTH, F32, 1024, 1024, "pool_in")
        h_b = _pool_mixer(u_pool, pool_w_bf, p_scale, l, batch, seq)

        u_conv = _in_proj(xb, w_in_r, l, OFF_CONV, 2 * CONV_WIDTH, F32, 1024, 1024, "conv_in")
        h_c = _conv_mixer(u_conv, conv_dw_p, c_b, c_g, c_beta, l, batch, seq)

        qkv = _dil_proj(xb, w_in_r, l, dil_tabs, tm=1024)
        h_d = _dil_attention(qkv, batch, seq)

        merged = _merge(xb, (h_a, h_b, h_c, h_d), w_in_r, w_projs, b_gate, l)
        xf, xb = _proj_ln(merged, w_out, xf, g1, b1, l, alpha, tm=512, tk=d_model, name="mix_out_ln1")

        hid = _ffn_up(xb, ffn_w_up, ffn_dw, f_b, l, seq)
        xf, xb = _proj_ln(hid, w_down, xf, g2, b2, l, alpha, tm=512, tk=ffn_dim // 4, name="ffn_down_ln2")
    return xf.reshape(batch, seq, d_model)
```

```python
import functools

import jax
import jax.numpy as jnp
from jax import lax
from jax.experimental import pallas as pl
from jax.experimental.pallas import tpu as pltpu

MLA_HEADS = 8
MLA_LORA = 512
MLA_NOPE = 128
MLA_ROPE = 64
MLA_V = 128
MLA_QK = MLA_NOPE + MLA_ROPE
POOL_WINDOWS = (2, 4, 8, 16)
POOL_GROUP = 256
POOL_WIDTH = 1024
CONV_WIDTH = 1024
CONV_K = 31
DIL_CONFIGS = ((128, 1), (512, 4), (2048, 16))
DIL_HEADS = 8
DIL_DIM = 128
DIL_WIDTH = DIL_HEADS * DIL_DIM
DIL_ROT = DIL_DIM // 4
DIL_BLOCK = 128
N_BRANCHES = 4
ROPE_THETA = 500000.0
FFN_K = 3
LN_EPS = 1e-5
RMS_EPS = 1e-6
NEG_INF = -1e30
LOG2_E = 1.4426950408889634

OFF_POOL = 0
OFF_CONV = OFF_POOL + POOL_WIDTH
OFF_DIL = OFF_CONV + 2 * CONV_WIDTH
OFF_GATE = OFF_DIL + len(DIL_CONFIGS) * 3 * DIL_WIDTH

LANES = 128
V7X_VMEM_BYTES = 64 * 1024 * 1024
VMEM_CAP = V7X_VMEM_BYTES - 8 * 1024 * 1024

BF16 = jnp.bfloat16
F32 = jnp.float32


def _params(semantics, block_bytes):
    limit = min(VMEM_CAP, max(32 * 1024 * 1024, 2 * block_bytes + 16 * 1024 * 1024))
    return pltpu.CompilerParams(dimension_semantics=semantics, vmem_limit_bytes=limit)


def _nbytes(shape, dtype):
    n = 1
    for s in shape:
        n *= s
    return n * jnp.dtype(dtype).itemsize


def _dot(a, b):
    return jnp.dot(a, b, preferred_element_type=F32)


def _dot_nt(a, b):
    return lax.dot_general(a, b, (((1,), (1,)), ((), ())), preferred_element_type=F32)


def _layer_spec(block, layer, index_map):
    return pl.BlockSpec((None,) + tuple(block), lambda *g: (layer,) + tuple(index_map(*g)))


def _rope_lanes(x, cos, sin_lo, sin_hi, half):
    return (x * cos + pltpu.roll(x, LANES - half, 1) * sin_lo
            + pltpu.roll(x, half, 1) * sin_hi)


def _rope_table_kernel(pos_ref, c_ref, cos_a, lo_a, hi_a, cos_b, lo_b, hi_b):
    pos = pos_ref[...]
    for base, (co, lo, hi) in ((0, (cos_a, lo_a, hi_a)), (3, (cos_b, lo_b, hi_b))):
        ang = pos * c_ref[base:base + 1, :]
        sn = jnp.sin(ang)
        co[...] = jnp.cos(ang)
        lo[...] = sn * c_ref[base + 1:base + 2, :]
        hi[...] = sn * c_ref[base + 2:base + 3, :]


def _lane_pattern(rot_dim):
    half = rot_dim // 2
    inv_freq = ROPE_THETA ** (-jnp.arange(half, dtype=F32) * 2.0 / rot_dim)
    lane = jnp.arange(LANES)
    freq = jnp.where(lane < rot_dim, jnp.tile(inv_freq, LANES // half), 0.0)
    lo = jnp.where(lane < half, -1.0, 0.0)
    hi = jnp.where((lane >= half) & (lane < rot_dim), 1.0, 0.0)
    return [freq.astype(F32), lo.astype(F32), hi.astype(F32)]


def _rope_tables(positions):
    m = positions.size
    pos = jnp.broadcast_to(positions.reshape(m, 1).astype(F32), (m, LANES))
    consts = jnp.stack(_lane_pattern(MLA_ROPE) + _lane_pattern(DIL_ROT)
                       + [jnp.zeros((LANES,), F32)] * 2)
    tm = 1024
    spec = pl.BlockSpec((tm, LANES), lambda i: (i, 0))
    out = jax.ShapeDtypeStruct((m, LANES), F32)
    return pl.pallas_call(
        _rope_table_kernel, grid=(m // tm,),
        in_specs=[spec, pl.BlockSpec((8, LANES), lambda i: (0, 0))],
        out_specs=[spec] * 6, out_shape=[out] * 6, name="rope_tables",
        compiler_params=_params(("parallel",), 7 * _nbytes((tm, LANES), F32)),
    )(pos, consts)


def _mm_kernel(a_ref, w_ref, o_ref):
    o_ref[...] = _dot(a_ref[...], w_ref[...]).astype(o_ref.dtype)


def _in_proj(a, w_in, layer, col_off, n, out_dtype, tm, tn, name):
    m, k = a.shape
    c0 = col_off // tn
    blk = (_nbytes((tm, k), a.dtype) + _nbytes((k, tn), w_in.dtype) + _nbytes((tm, tn), out_dtype))
    return pl.pallas_call(
        _mm_kernel, grid=(n // tn, m // tm),
        in_specs=[pl.BlockSpec((tm, k), lambda j, i: (i, 0)),
                  _layer_spec((k, tn), layer, lambda j, i: (0, c0 + j))],
        out_specs=pl.BlockSpec((tm, tn), lambda j, i: (i, j)),
        out_shape=jax.ShapeDtypeStruct((m, n), out_dtype), name=name,
        compiler_params=_params(("parallel", "parallel"), blk),
    )(a, w_in)


MXU_COLS = 256


def _skewed(i, nt, bufs, compute_chunks, epilogue_chunks):
    def run(cur, prv, do_compute, do_epilogue):
        cs = compute_chunks(cur, prv) if do_compute else []
        es = epilogue_chunks(prv) if do_epilogue else []
        per = -(-len(es) // len(cs)) if cs else len(es)
        for k in range(max(len(cs), 1)):
            if cs:
                cs[k]()
            for e in es[k * per:(k + 1) * per]:
                e()

    parity = lax.rem(i, 2)
    middle = (i > 0) & (i < nt)
    pl.when(i == 0)(lambda: run(bufs[0], None, True, False))
    pl.when(middle & (parity == 0))(lambda: run(bufs[0], bufs[1], True, True))
    pl.when(middle & (parity == 1))(lambda: run(bufs[1], bufs[0], True, True))
    pl.when(i == nt)(lambda: run(None, bufs[1 - nt % 2], False, True))


def _skewed_in(tm_width, nt):
    return pl.BlockSpec(tm_width, lambda j, i: (jnp.minimum(i, nt - 1), 0))


def _dil_proj_kernel(a_ref, w_ref, cos_ref, lo_ref, hi_ref, o_ref, acc_a, acc_b, *, nt):
    tm = o_ref.shape[0]
    is_v = lax.rem(pl.program_id(0), 3) == 2

    def compute_chunks(cur, prv):
        def chunk(c):
            cs = slice(c * MXU_COLS, (c + 1) * MXU_COLS)
            cur[:, cs] = _dot(a_ref[...], w_ref[:, cs])
        return [functools.partial(chunk, c) for c in range(DIL_WIDTH // MXU_COLS)]

    def epilogue_chunks(prv):
        half = tm // 2

        def chunk(h, r):
            rows = slice(r * half, (r + 1) * half)
            sl = slice(h * DIL_DIM, (h + 1) * DIL_DIM)
            cos = jnp.where(is_v, 1.0, cos_ref[rows, :])
            lo = jnp.where(is_v, 0.0, lo_ref[rows, :])
            hi = jnp.where(is_v, 0.0, hi_ref[rows, :])
            o_ref[rows, sl] = _rope_lanes(prv[rows, sl], cos, lo, hi, DIL_ROT // 2).astype(o_ref.dtype)
        return [functools.partial(chunk, h, r) for h in range(DIL_HEADS) for r in range(2)]

    _skewed(pl.program_id(1), nt, (acc_a, acc_b), compute_chunks, epilogue_chunks)


def _dil_proj(x_bf, w_in, layer, tabs, tm):
    m, k = x_bf.shape
    n = len(DIL_CONFIGS) * 3 * DIL_WIDTH
    tn = DIL_WIDTH
    nt = m // tm
    c0 = OFF_DIL // tn
    prev_row = lambda j, i: (jnp.maximum(i - 1, 0), 0)
    tab_spec = pl.BlockSpec((tm, LANES), prev_row)
    blk = (_nbytes((tm, k), BF16) + _nbytes((k, tn), BF16) + _nbytes((tm, tn), BF16)
           + 3 * _nbytes((tm, LANES), F32) + _nbytes((tm, tn), F32))
    return pl.pallas_call(
        functools.partial(_dil_proj_kernel, nt=nt), grid=(n // tn, nt + 1),
        in_specs=[_skewed_in((tm, k), nt),
                  _layer_spec((k, tn), layer, lambda j, i: (0, c0 + j)),
                  tab_spec, tab_spec, tab_spec],
        out_specs=pl.BlockSpec((tm, tn), lambda j, i: (jnp.maximum(i - 1, 0), j)),
        out_shape=jax.ShapeDtypeStruct((m, n), BF16),
        scratch_shapes=[pltpu.VMEM((tm, tn), F32)] * 2, name="dil_proj",
        compiler_params=_params(("parallel", "arbitrary"), blk),
    )(x_bf, w_in, *tabs)


def _latent_kernel(x_ref, wlat_ref, gq_ref, gkv_ref, wuq_ref, wukv_ref,
                   cos_ref, lo_ref, hi_ref, q_ref, k_ref, v_ref):
    lat = _dot(x_ref[...], wlat_ref[...])
    cos, lo, hi = cos_ref[...], lo_ref[...], hi_ref[...]

    def rms(c, g):
        return c * lax.rsqrt(jnp.mean(jnp.square(c), axis=-1, keepdims=True) + RMS_EPS) * g

    def rope(v):
        return _rope_lanes(v, cos, lo, hi, MLA_ROPE // 2)

    cq = rms(lat[:, :MLA_LORA], gq_ref[...]).astype(BF16)
    ckv = rms(lat[:, MLA_LORA:2 * MLA_LORA], gkv_ref[...]).astype(BF16)
    kpe = rope(lat[:, 2 * MLA_LORA:]).astype(BF16)
    q = _dot(cq, wuq_ref[...])
    kv = _dot(ckv, wukv_ref[...])
    for h in range(MLA_HEADS):
        b0 = h * 2 * LANES
        q_ref[:, b0:b0 + LANES] = q[:, b0:b0 + LANES].astype(BF16)
        q_ref[:, b0 + LANES:b0 + 2 * LANES] = rope(q[:, b0 + LANES:b0 + 2 * LANES]).astype(BF16)
        k_ref[:, b0:b0 + LANES] = kv[:, h * LANES:(h + 1) * LANES].astype(BF16)
        k_ref[:, b0 + LANES:b0 + 2 * LANES] = kpe
    v_ref[...] = kv[:, MLA_HEADS * MLA_NOPE:].astype(BF16)


def _latent(x_bf, w_lat, gq, gkv, w_uq, w_ukv, layer, tabs, tm):
    m, k = x_bf.shape
    nl = w_lat.shape[2]
    hq = MLA_HEADS * 2 * LANES
    hv = MLA_HEADS * MLA_V
    zero2 = lambda i: (0, 0)
    row = lambda w: pl.BlockSpec((tm, w), lambda i: (i, 0))
    blk = (_nbytes((tm, k), BF16) + _nbytes((k, nl), BF16) + 2 * _nbytes((MLA_LORA, hq), BF16)
           + 2 * _nbytes((tm, hq), BF16) + _nbytes((tm, hv), BF16) + 3 * _nbytes((tm, LANES), F32))
    return pl.pallas_call(
        _latent_kernel, grid=(m // tm,),
        in_specs=[row(k), _layer_spec((k, nl), layer, zero2),
                  _layer_spec((1, MLA_LORA), layer, zero2), _layer_spec((1, MLA_LORA), layer, zero2),
                  _layer_spec((MLA_LORA, hq), layer, zero2), _layer_spec((MLA_LORA, hq), layer, zero2),
                  row(LANES), row(LANES), row(LANES)],
        out_specs=[row(hq), row(hq), row(hv)],
        out_shape=[jax.ShapeDtypeStruct((m, hq), BF16), jax.ShapeDtypeStruct((m, hq), BF16),
                   jax.ShapeDtypeStruct((m, hv), BF16)],
        name="mla_latent",
        compiler_params=_params(("parallel",), blk),
    )(x_bf, w_lat, gq, gkv, w_uq, w_ukv, *tabs)


def _mla_attn_kernel(q_ref, k_ref, v_ref, o_ref, *, tq, scale):
    s = q_ref.shape[0]
    c = scale * LOG2_E
    diag = (lax.broadcasted_iota(jnp.int32, (tq, tq), 1)
            <= lax.broadcasted_iota(jnp.int32, (tq, tq), 0))
    for qi in range(s // tq):
        s0, s1 = qi * tq, (qi + 1) * tq
        q = q_ref[s0:s1, :]
        sd = jnp.where(diag, _dot_nt(q, k_ref[s0:s1, :]), NEG_INF)
        mx = jnp.max(sd, axis=-1, keepdims=True)
        if qi > 0:
            so = _dot_nt(q, k_ref[:s0, :])
            mx = jnp.maximum(mx, jnp.max(so, axis=-1, keepdims=True))
            po = jnp.exp2((so - mx) * c)
        pd = jnp.exp2((sd - mx) * c)
        den = jnp.sum(pd, axis=-1, keepdims=True)
        o = _dot(pd.astype(BF16), v_ref[s0:s1, :])
        if qi > 0:
            den = den + jnp.sum(po, axis=-1, keepdims=True)
            o = o + _dot(po.astype(BF16), v_ref[:s0, :])
        o_ref[s0:s1, :] = (o / den).astype(o_ref.dtype)


def _mla_attention(q, k, v, batch, seq, tq=256):
    m = q.shape[0]
    dq = 2 * LANES
    blk = 2 * _nbytes((seq, dq), BF16) + 2 * _nbytes((seq, MLA_V), BF16)
    return pl.pallas_call(
        functools.partial(_mla_attn_kernel, tq=tq, scale=MLA_QK ** -0.5),
        grid=(batch, MLA_HEADS),
        in_specs=[pl.BlockSpec((seq, dq), lambda b, h: (b, h)),
                  pl.BlockSpec((seq, dq), lambda b, h: (b, h)),
                  pl.BlockSpec((seq, MLA_V), lambda b, h: (b, h))],
        out_specs=pl.BlockSpec((seq, MLA_V), lambda b, h: (b, h)),
        out_shape=jax.ShapeDtypeStruct((m, MLA_HEADS * MLA_V), BF16), name="mla_attention",
        compiler_params=_params(("parallel", "parallel"), blk + 4 * _nbytes((tq, seq), F32)),
    )(q, k, v)


DIL_UNROLL = 16


def _dil_attn_kernel(q1, k1, v1, q2, k2, v2, q3, k3, v3, o_ref, stage, o_acc, lse_acc, *, seq, scale):
    nb = DIL_BLOCK
    diff = (lax.broadcasted_iota(jnp.int32, (nb, 2 * nb), 0)
            - lax.broadcasted_iota(jnp.int32, (nb, 2 * nb), 1))
    band = (diff + nb >= 0) & (diff <= 0)
    causal = (lax.broadcasted_iota(jnp.int32, (nb, nb), 1)
              <= lax.broadcasted_iota(jnp.int32, (nb, nb), 0))

    def attend_many(gi, blocks):
        scs = [jnp.where(mask, _dot_nt(q, k), NEG_INF) for _, q, k, _, mask in blocks]
        mxs = [jnp.max(sc, axis=-1, keepdims=True) for sc in scs]
        es = [jnp.exp2((sc - mx) * (scale * LOG2_E)) for sc, mx in zip(scs, mxs)]
        dens = [jnp.sum(e, axis=-1, keepdims=True) for e in es]
        ps = [(e / den).astype(BF16) for e, den in zip(es, dens)]
        outs = [_dot(p, blk[3]) for p, blk in zip(ps, blocks)]
        for blk, o, mx, den in zip(blocks, outs, mxs, dens):
            o_acc[gi, blk[0], :] = o
            lse_acc[gi, blk[0], :] = jnp.broadcast_to(mx * scale + jnp.log(den), (nb, LANES))

    def block_body(it, carry):
        blocks = []
        for u in range(DIL_UNROLL):
            blk = it * DIL_UNROLL + u
            q0 = pl.multiple_of(blk * nb, nb)
            if u == 0:
                k0 = pl.multiple_of(jnp.maximum(blk - 1, 0) * nb, nb)
                rel = diff + (q0 - k0)
                mask = (rel >= 0) & (rel <= nb)
            else:
                k0 = pl.multiple_of(blk * nb - nb, nb)
                mask = band
            blocks.append((pl.ds(q0, nb), q1[pl.ds(q0, nb), :], k1[pl.ds(k0, 2 * nb), :],
                           v1[pl.ds(k0, 2 * nb), :], mask))
        attend_many(0, blocks)
        return carry

    lax.fori_loop(0, seq // (nb * DIL_UNROLL), block_body, 0)

    for gi, (q_ref, k_ref, v_ref) in ((1, (q2, k2, v2)), (2, (q3, k3, v3))):
        dil = DIL_CONFIGS[gi][1]
        length = seq // dil
        nblk = length // nb
        per_iter = DIL_UNROLL // nblk
        stage[0] = q_ref[...].astype(F32)
        stage[1] = k_ref[...].astype(F32)
        stage[2] = v_ref[...].astype(F32)

        for it in range(dil // per_iter):
            blocks = []
            for r in range(it * per_iter, (it + 1) * per_iter):
                qc, kc, vc = [stage[j, pl.ds(r, length, stride=dil), :].astype(BF16) for j in range(3)]
                for blk in range(nblk):
                    rows = pl.ds(r + blk * nb * dil, nb, stride=dil)
                    q_blk = qc[blk * nb:(blk + 1) * nb]
                    if blk == 0:
                        blocks.append((rows, q_blk, kc[0:nb], vc[0:nb], causal))
                    else:
                        blocks.append((rows, q_blk, kc[(blk - 1) * nb:(blk + 1) * nb],
                                       vc[(blk - 1) * nb:(blk + 1) * nb], band))
            attend_many(gi, blocks)

    chunk = 2 * nb

    def merge_body(c, carry):
        rows = pl.ds(pl.multiple_of(c * chunk, chunk), chunk)
        la, lb, lc = lse_acc[0, rows, :], lse_acc[1, rows, :], lse_acc[2, rows, :]
        mx = jnp.maximum(jnp.maximum(la, lb), lc)
        ea, eb, ec = jnp.exp(la - mx), jnp.exp(lb - mx), jnp.exp(lc - mx)
        den = ea + eb + ec
        out = ((ea / den) * o_acc[0, rows, :] + (eb / den) * o_acc[1, rows, :]
               + (ec / den) * o_acc[2, rows, :])
        o_ref[rows, :] = out.astype(o_ref.dtype)
        return carry

    lax.fori_loop(0, seq // chunk, merge_body, 0)


def _dil_attention(qkv, batch, seq):
    m = qkv.shape[0]
    col_blocks = DIL_WIDTH // LANES
    specs = [pl.BlockSpec((seq, LANES), lambda b, h, c=(g * 3 + j) * col_blocks: (b, c + h))
             for g in range(len(DIL_CONFIGS)) for j in range(3)]
    blk = 10 * _nbytes((seq, LANES), BF16)
    scratch = 9 * _nbytes((seq, LANES), F32)
    return pl.pallas_call(
        functools.partial(_dil_attn_kernel, seq=seq, scale=DIL_DIM ** -0.5),
        grid=(batch, DIL_HEADS),
        in_specs=specs,
        out_specs=pl.BlockSpec((seq, LANES), lambda b, h: (b, h)),
        out_shape=jax.ShapeDtypeStruct((m, DIL_WIDTH), BF16),
        scratch_shapes=[pltpu.VMEM((3, seq, LANES), F32)] * 3,
        name="dil_attention",
        compiler_params=_params(("parallel", "parallel"), blk + scratch // 2),
    )(*([qkv] * 9))


POOL_HALO = 16


def _pool_kernel(cur_ref, halo_ref, w_ref, scale_ref, o_ref, buf, *, ts):
    i = pl.program_id(1)
    buf[0:POOL_HALO, :] = jnp.where(i > 0, halo_ref[...], 0.0)
    buf[POOL_HALO:, :] = cur_ref[...]
    t = i * ts + lax.broadcasted_iota(jnp.int32, (ts, 1), 0)
    for g, win in enumerate(POOL_WINDOWS):
        sl = slice(g * POOL_GROUP, (g + 1) * POOL_GROUP)
        tok = buf[POOL_HALO:POOL_HALO + ts, sl]
        acc = tok
        for j in range(1, win):
            acc = acc + buf[POOL_HALO - j:POOL_HALO - j + ts, sl]
        cnt = jnp.minimum(t + 1, win).astype(F32)
        d = (acc / cnt - tok).astype(BF16)
        o_ref[:, sl] = (_dot(d, w_ref[g]) * scale_ref[:, sl]).astype(o_ref.dtype)


def _pool_mixer(u_pool, w_grp, scale, layer, batch, seq, ts=256):
    m, c = u_pool.shape
    nt = seq // ts
    hb = ts // POOL_HALO
    blk = (_nbytes((ts, c), F32) + _nbytes((POOL_HALO, c), F32) + _nbytes(w_grp.shape[1:], BF16)
           + _nbytes((ts, c), BF16) + _nbytes((ts + POOL_HALO, c), F32))
    return pl.pallas_call(
        functools.partial(_pool_kernel, ts=ts), grid=(batch, nt),
        in_specs=[pl.BlockSpec((ts, c), lambda b, i: (b * nt + i, 0)),
                  pl.BlockSpec((POOL_HALO, c), lambda b, i: ((b * nt + i) * hb - jnp.minimum(i, 1), 0)),
                  _layer_spec(w_grp.shape[1:], layer, lambda b, i: (0, 0, 0)),
                  _layer_spec((1, c), layer, lambda b, i: (0, 0))],
        out_specs=pl.BlockSpec((ts, c), lambda b, i: (b * nt + i, 0)),
        out_shape=jax.ShapeDtypeStruct((m, c), BF16),
        scratch_shapes=[pltpu.VMEM((ts + POOL_HALO, c), F32)], name="pool_mixer",
        compiler_params=_params(("parallel", "parallel"), blk),
    )(u_pool, u_pool, w_grp, scale)


CONV_HALO = 32


SUBLANES = 8
CONV_ROWS = 64


def _conv_kernel(cur_ref, halo_ref, w_ref, b_ref, g_ref, beta_ref, o_ref, buf, shifted, hbuf, *, ts):
    i = pl.program_id(1)
    c = CONV_WIDTH

    def glu(u):
        return u[:, :c] * jax.nn.sigmoid(u[:, c:])

    buf[0:CONV_HALO, :] = jnp.where(i > 0, glu(halo_ref[...]), 0.0)

    def glu_body(j, carry):
        r0 = pl.multiple_of(j * CONV_HALO, CONV_HALO)
        buf[pl.ds(CONV_HALO + r0, CONV_HALO), :] = glu(cur_ref[pl.ds(r0, CONV_HALO), :])
        return carry

    lax.fori_loop(0, ts // CONV_HALO, glu_body, 0)
    span = ts + CONV_HALO - SUBLANES
    for r in range(1, SUBLANES):
        shifted[r - 1, 0:span, :] = buf[r:r + span, :]
    off = CONV_HALO - (CONV_K - 1)
    for cc in range(c // LANES):
        sl = slice(cc * LANES, (cc + 1) * LANES)
        for r0 in range(0, ts, CONV_ROWS):
            acc = jnp.zeros((CONV_ROWS, LANES), F32)
            for k in range(CONV_K):
                r = (off + k) % SUBLANES
                base = off + k - r + r0
                win = (buf[base:base + CONV_ROWS, sl] if r == 0
                       else shifted[r - 1, base:base + CONV_ROWS, sl])
                acc = acc + w_ref[k:k + 1, sl] * win
            hbuf[r0:r0 + CONV_ROWS, sl] = acc + b_ref[:, sl]
    ln_rows = 4 * CONV_HALO

    def ln_body(j, carry):
        rows = pl.ds(pl.multiple_of(j * ln_rows, ln_rows), ln_rows)
        h = hbuf[rows, :]
        mu = jnp.mean(h, axis=-1, keepdims=True)
        var = jnp.mean(jnp.square(h - mu), axis=-1, keepdims=True)
        y = (h - mu) * lax.rsqrt(var + LN_EPS) * g_ref[...] + beta_ref[...]
        o_ref[rows, :] = (y * jax.nn.sigmoid(y)).astype(o_ref.dtype)
        return carry

    lax.fori_loop(0, ts // ln_rows, ln_body, 0)


def _conv_mixer(u_conv, w_dw, b_dw, ln_g, ln_b, layer, batch, seq, ts=256):
    m, c2 = u_conv.shape
    c = c2 // 2
    nt = seq // ts
    hb = ts // CONV_HALO
    vec = _layer_spec((1, c), layer, lambda b, i: (0, 0))
    blk = (_nbytes((ts, c2), F32) + _nbytes((CONV_HALO, c2), F32) + _nbytes(w_dw.shape[1:], F32)
           + _nbytes((ts, c), BF16) + 2 * _nbytes((ts + CONV_HALO, c), F32))
    return pl.pallas_call(
        functools.partial(_conv_kernel, ts=ts), grid=(batch, nt),
        in_specs=[pl.BlockSpec((ts, c2), lambda b, i: (b * nt + i, 0)),
                  pl.BlockSpec((CONV_HALO, c2), lambda b, i: ((b * nt + i) * hb - jnp.minimum(i, 1), 0)),
                  _layer_spec(w_dw.shape[1:], layer, lambda b, i: (0, 0)), vec, vec, vec],
        out_specs=pl.BlockSpec((ts, c), lambda b, i: (b * nt + i, 0)),
        out_shape=jax.ShapeDtypeStruct((m, c), BF16),
        scratch_shapes=[pltpu.VMEM((ts + CONV_HALO, c), F32),
                        pltpu.VMEM((SUBLANES - 1, ts + CONV_HALO, c), F32), pltpu.VMEM((ts, c), F32)],
        name="conv_mixer",
        compiler_params=_params(("parallel", "parallel"), blk + 4 * _nbytes((ts + CONV_HALO, c), F32)),
    )(u_conv, u_conv, w_dw, b_dw, ln_g, ln_b)


def _merge_kernel(x_ref, ha, hb, hc, hd, g0, g1, g2, g3, pa, pb, pc, pd, bias_ref, o_ref):
    x = x_ref[...]
    acc = None
    for b, (h_ref, wg_ref, wp_ref) in enumerate(((ha, g0, pa), (hb, g1, pb), (hc, g2, pc), (hd, g3, pd))):
        gate = jax.nn.sigmoid(_dot(x, wg_ref[...]) + bias_ref[b:b + 1, :])
        term = gate * _dot(h_ref[...], wp_ref[...])
        acc = term if acc is None else acc + term
    o_ref[...] = acc.astype(o_ref.dtype)


def _merge(x_bf, hs, w_in, w_projs, b_gate, layer, tm=512, tn=512):
    m, d = x_bf.shape
    kb = hs[0].shape[1]
    nj = d // tn
    c0 = OFF_GATE // tn
    gate_spec = lambda b: _layer_spec((d, tn), layer, lambda j, i: (0, c0 + b * nj + j))
    blk = (_nbytes((tm, d), BF16) + 4 * _nbytes((tm, kb), BF16) + 4 * _nbytes((d, tn), BF16)
           + 4 * _nbytes((kb, tn), BF16) + _nbytes((tm, tn), BF16) + 4 * _nbytes((tm, tn), F32))
    return pl.pallas_call(
        _merge_kernel, grid=(nj, m // tm),
        in_specs=([pl.BlockSpec((tm, d), lambda j, i: (i, 0))]
                  + [pl.BlockSpec((tm, kb), lambda j, i: (i, 0))] * 4
                  + [gate_spec(b) for b in range(N_BRANCHES)]
                  + [_layer_spec((kb, tn), layer, lambda j, i: (0, j))] * 4
                  + [_layer_spec((N_BRANCHES, tn), layer, lambda j, i: (0, j))]),
        out_specs=pl.BlockSpec((tm, tn), lambda j, i: (i, j)),
        out_shape=jax.ShapeDtypeStruct((m, d), BF16), name="gated_merge",
        compiler_params=_params(("parallel", "parallel"), blk),
    )(x_bf, *hs, w_in, w_in, w_in, w_in, *w_projs, b_gate)


def _proj_ln_kernel(a_ref, w_ref, res_ref, g_ref, b_ref, of_ref, ob_ref, acc_ref, *, alpha):
    k = pl.program_id(1)

    @pl.when(k == 0)
    def _():
        acc_ref[...] = jnp.zeros_like(acc_ref)

    acc_ref[...] += _dot(a_ref[...], w_ref[...])

    @pl.when(k == pl.num_programs(1) - 1)
    def _():
        y = alpha * res_ref[...] + acc_ref[...]
        mu = jnp.mean(y, axis=-1, keepdims=True)
        var = jnp.mean(jnp.square(y - mu), axis=-1, keepdims=True)
        out = (y - mu) * lax.rsqrt(var + LN_EPS) * g_ref[...] + b_ref[...]
        of_ref[...] = out
        ob_ref[...] = out.astype(BF16)


def _proj_ln(a, w, res, g, b, layer, alpha, tm, tk, name):
    m, k = a.shape
    d = w.shape[2]
    vec = _layer_spec((1, d), layer, lambda i, kk: (0, 0))
    row = pl.BlockSpec((tm, d), lambda i, kk: (i, 0))
    blk = (_nbytes((tm, tk), BF16) + _nbytes((tk, d), BF16) + 2 * _nbytes((tm, d), F32)
           + _nbytes((tm, d), BF16) + _nbytes((tm, d), F32))
    return pl.pallas_call(
        functools.partial(_proj_ln_kernel, alpha=alpha), grid=(m // tm, k // tk),
        in_specs=[pl.BlockSpec((tm, tk), lambda i, kk: (i, kk)),
                  _layer_spec((tk, d), layer, lambda i, kk: (kk, 0)), row, vec, vec],
        out_specs=[row, row],
        out_shape=[jax.ShapeDtypeStruct((m, d), F32), jax.ShapeDtypeStruct((m, d), BF16)],
        scratch_shapes=[pltpu.VMEM((tm, d), F32)], name=name,
        compiler_params=_params(("parallel", "arbitrary"), blk),
    )(a, w, res, g, b)


FFN_HALO = 8


def _ffn_up_kernel(x_ref, wv_ref, wg_ref, dwv_ref, dwg_ref, bv_ref, bg_ref, o_ref,
                   wv_bf, wg_bf, bufv, bufg, *, tm, tiles_per_seq):
    i = pl.program_id(1)
    first = lax.rem(i, tiles_per_seq) == 0

    @pl.when(i == 0)
    def _():
        wv_bf[...] = wv_ref[...].astype(BF16)
        wg_bf[...] = wg_ref[...].astype(BF16)

    @pl.when(first)
    def _():
        bufv[0:FFN_HALO, :] = jnp.zeros((FFN_HALO, bufv.shape[1]), F32)
        bufg[0:FFN_HALO, :] = jnp.zeros((FFN_HALO, bufg.shape[1]), F32)

    @pl.when(jnp.logical_not(first))
    def _():
        bufv[0:FFN_HALO, :] = bufv[tm:tm + FFN_HALO, :]
        bufg[0:FFN_HALO, :] = bufg[tm:tm + FFN_HALO, :]

    def conv(buf, w_ref, b_ref, cs, r0, n):
        full = buf[r0:r0 + n + FFN_HALO, cs]
        acc = None
        for k in range(FFN_K):
            back = FFN_K - 1 - k
            win = (pltpu.roll(full, back, 0) if back else full)[FFN_HALO:]
            term = w_ref[k:k + 1, cs] * win
            acc = term if acc is None else acc + term
        return acc + b_ref[:, cs]

    x = x_ref[...]
    for c in range(o_ref.shape[1] // MXU_COLS):
        cs = slice(c * MXU_COLS, (c + 1) * MXU_COLS)
        bufv[FFN_HALO:, cs] = _dot(x, wv_bf[:, cs])
        bufg[FFN_HALO:, cs] = _dot(x, wg_bf[:, cs])
        val = conv(bufv, dwv_ref, bv_ref, cs, 0, tm)
        gate = conv(bufg, dwg_ref, bg_ref, cs, 0, tm)
        o_ref[:, cs] = (val * (gate * jax.nn.sigmoid(gate))).astype(o_ref.dtype)


def _ffn_up(x_bf, w_up, dw, dw_b, layer, seq, tm=1024, tn=512):
    m, d = x_bf.shape
    f = w_up.shape[2] // 2
    nj = f // tn
    blk = (_nbytes((tm, d), BF16) + 2 * _nbytes((d, tn), F32) + _nbytes((d, tn), BF16)
           + _nbytes((tm, tn), BF16) + _nbytes((tm + FFN_HALO, tn), F32))
    return pl.pallas_call(
        functools.partial(_ffn_up_kernel, tm=tm, tiles_per_seq=seq // tm), grid=(nj, m // tm),
        in_specs=[pl.BlockSpec((tm, d), lambda j, i: (i, 0)),
                  _layer_spec((d, tn), layer, lambda j, i: (0, j)),
                  _layer_spec((d, tn), layer, lambda j, i: (0, nj + j)),
                  _layer_spec((FFN_K, tn), layer, lambda j, i: (0, j)),
                  _layer_spec((FFN_K, tn), layer, lambda j, i: (0, nj + j)),
                  _layer_spec((1, tn), layer, lambda j, i: (0, j)),
                  _layer_spec((1, tn), layer, lambda j, i: (0, nj + j))],
        out_specs=pl.BlockSpec((tm, tn), lambda j, i: (i, j)),
        out_shape=jax.ShapeDtypeStruct((m, f), BF16),
        scratch_shapes=[pltpu.VMEM((d, tn), BF16)] * 2 + [pltpu.VMEM((tm + FFN_HALO, tn), F32)] * 2,
        name="ffn_up",
        compiler_params=_params(("parallel", "arbitrary"), blk),
    )(x_bf, w_up, w_up, dw, dw, dw_b, dw_b)


def _prep_in_proj(w_in):
    pe1 = 2 * MLA_LORA + MLA_ROPE
    w_lat = jnp.pad(w_in[:, :, :pe1], ((0, 0), (0, 0), (0, LANES - MLA_ROPE))).astype(BF16)
    return w_lat, w_in.astype(BF16)[:, :, pe1:]


def _prep_mla_up(mla_w_uq, mla_w_ukv):
    depth = mla_w_uq.shape[0]
    w_uq = jnp.pad(mla_w_uq.reshape(depth, MLA_LORA, MLA_HEADS, MLA_QK),
                   ((0, 0), (0, 0), (0, 0), (0, 2 * LANES - MLA_QK)))
    w_uq = w_uq.reshape(depth, MLA_LORA, -1).astype(BF16)
    ukv = mla_w_ukv.reshape(depth, MLA_LORA, MLA_HEADS, MLA_NOPE + MLA_V)
    w_ukv = jnp.concatenate([ukv[..., :MLA_NOPE].reshape(depth, MLA_LORA, -1),
                             ukv[..., MLA_NOPE:].reshape(depth, MLA_LORA, -1)], axis=2).astype(BF16)
    return w_uq, w_ukv


def kernel(x, positions, w_in, b_gate, mla_gq, mla_gkv, mla_w_uq, mla_w_ukv, mla_w_proj, pool_w, pool_scale, pool_w_proj, conv_dw, conv_dw_b, conv_ln_g, conv_ln_b, conv_w_proj, dil_w_proj, mix_w_out, ln1_g, ln1_b, ffn_w_up, ffn_dw, ffn_dw_b, ffn_w_down, ln2_g, ln2_b):
    batch, seq, d_model = x.shape
    depth = w_in.shape[0]
    m = batch * seq
    alpha = (2 * depth) ** 0.25
    ffn_dim = ffn_w_down.shape[1]

    tabs = _rope_tables(positions)
    mla_tabs, dil_tabs = tabs[:3], tabs[3:]

    w_lat, w_in_r = _prep_in_proj(w_in)
    w_uq, w_ukv = _prep_mla_up(mla_w_uq, mla_w_ukv)
    w_projs = tuple(w.astype(BF16) for w in (mla_w_proj, pool_w_proj, conv_w_proj, dil_w_proj))
    w_out = mix_w_out.astype(BF16)
    w_down = ffn_w_down.astype(BF16)
    pool_w_bf = pool_w.astype(BF16)
    conv_dw_p = jnp.pad(conv_dw, ((0, 0), (0, 1), (0, 0)))
    vec = lambda v: v[:, None, :]
    gq, gkv, p_scale = vec(mla_gq), vec(mla_gkv), vec(pool_scale)
    c_b, c_g, c_beta = vec(conv_dw_b), vec(conv_ln_g), vec(conv_ln_b)
    g1, b1, g2, b2 = vec(ln1_g), vec(ln1_b), vec(ln2_g), vec(ln2_b)
    f_b = vec(ffn_dw_b)

    xf = x.reshape(m, d_model)
    xb = xf.astype(BF16)
    for l in range(depth):
        q, k, v = _latent(xb, w_lat, gq, gkv, w_uq, w_ukv, l, mla_tabs, tm=512)
        h_a = _mla_attention(q, k, v, batch, seq)

        u_pool = _in_proj(xb, w_in_r, l, OFF_POOL, POOL_WIDTH, F32, 1024, 1024, "pool_in")
        h_b = _pool_mixer(u_pool, pool_w_bf, p_scale, l, batch, seq)

        u_conv = _in_proj(xb, w_in_r, l, OFF_CONV, 2 * CONV_WIDTH, F32, 1024, 1024, "conv_in")
        h_c = _conv_mixer(u_conv, conv_dw_p, c_b, c_g, c_beta, l, batch, seq)

        qkv = _dil_proj(xb, w_in_r, l, dil_tabs, tm=1024)
        h_d = _dil_attention(qkv, batch, seq)

        merged = _merge(xb, (h_a, h_b, h_c, h_d), w_in_r, w_projs, b_gate, l)
        xf, xb = _proj_ln(merged, w_out, xf, g1, b1, l, alpha, tm=512, tk=d_model, name="mix_out_ln1")

        hid = _ffn_up(xb, ffn_w_up, ffn_dw, f_b, l, seq)
        xf, xb = _proj_ln(hid, w_down, xf, g2, b2, l, alpha, tm=512, tk=ffn_dim // 2, name="ffn_down_ln2")
    return xf.reshape(batch, seq, d_model)
```
